```python
import jax, jax.numpy as jnp
from jax import lax
import numpy as np

D_MODEL = 1024
BATCH = 2
SEQ = 16384
DEPTH = 4

HG_HEADS = 8
HG_DK = 128
HG_DV = 128
HG_WIDTH = HG_HEADS * HG_DK
HG_CHUNK = 64
ATT_Q_HEADS = 16
ATT_KV_HEADS = 4
ATT_GROUP = ATT_Q_HEADS // ATT_KV_HEADS
ATT_HEAD_DIM = 64
ATT_WIDTH = ATT_Q_HEADS * ATT_HEAD_DIM
ATT_KV_WIDTH = ATT_KV_HEADS * ATT_HEAD_DIM
WINDOW = 128
ATT_BLOCK = 128
ROPE_THETA = 500000.0
ROPE_DIM = ATT_HEAD_DIM // 4
FFN_HIDDEN = ((8 * D_MODEL // 3 + 255) // 256) * 256
EPS = 1e-6
MIN_F = 1e-30
IN_SIZES = (HG_WIDTH, HG_WIDTH, HG_WIDTH, HG_WIDTH, ATT_WIDTH, ATT_KV_WIDTH, ATT_KV_WIDTH, D_MODEL, D_MODEL)
IN_COLS = 4 * HG_WIDTH + ATT_WIDTH + 2 * ATT_KV_WIDTH + 2 * D_MODEL

kernel_name = "hgrn2_swa_sink_gated_hybrid"


def rmsnorm(x, g):
    xf = x.astype(jnp.float32)
    y = xf * lax.rsqrt(jnp.mean(xf * xf, axis=-1, keepdims=True) + EPS)
    return (y * g.astype(jnp.float32)).astype(x.dtype)


def split_cols(proj):
    points, acc = [], 0
    for s in IN_SIZES[:-1]:
        acc += s
        points.append(acc)
    return jnp.split(proj, points, axis=-1)


def rope_partial(x, pos):
    half = ROPE_DIM // 2
    inv = ROPE_THETA ** (-jnp.arange(half, dtype=jnp.float32) * 2.0 / ROPE_DIM)
    ang = pos.astype(jnp.float32)[:, None] * inv[None, :]
    cos = jnp.cos(ang)[None, :, None, :]
    sin = jnp.sin(ang)[None, :, None, :]
    xr = x[..., :ROPE_DIM].astype(jnp.float32)
    x1, x2 = xr[..., :half], xr[..., half:]
    rot = jnp.concatenate([x1 * cos - x2 * sin, x2 * cos + x1 * sin], axis=-1)
    return jnp.concatenate([rot.astype(x.dtype), x[..., ROPE_DIM:]], axis=-1)


def hgrn2_chunked(q, k, v, logf):
    B, S, H, DK = q.shape
    DV = v.shape[-1]
    C = HG_CHUNK
    n = S // C

    def to_chunks(t):
        return t.reshape(B, n, C, H, t.shape[-1]).transpose(1, 0, 3, 2, 4)

    qc, kc, vc, fc = to_chunks(q), to_chunks(k), to_chunks(v), to_chunks(logf)
    causal = jnp.tril(jnp.ones((C, C), dtype=bool))

    def step(state, inp):
        qi, ki, vi, fi = inp
        b = jnp.cumsum(fi, axis=2)
        o_inter = jnp.einsum('bhtk,bhkv->bhtv', qi * jnp.exp(b), state)
        diff = b[:, :, :, None, :] - b[:, :, None, :, :]
        decay = jnp.exp(jnp.where(causal[:, :, None], diff, -jnp.inf))
        scores = jnp.einsum('bhtk,bhsk,bhtsk->bhts', qi, ki, decay)
        o_intra = jnp.einsum('bhts,bhsv->bhtv', scores, vi)
        b_last = b[:, :, -1:, :]
        new_state = jnp.exp(b_last[:, :, 0, :])[..., None] * state + jnp.einsum(
            'bhsk,bhsv->bhkv', ki * jnp.exp(b_last - b), vi)
        return new_state, o_inter + o_intra

    init = jnp.zeros((B, H, DK, DV), jnp.float32)
    _, o = lax.scan(step, init, (qc, kc, vc, fc))
    return o.transpose(1, 0, 3, 2, 4).reshape(B, S, H, DV)


def swa_with_sinks(q, k, v, sinks):
    B, S, Hq, hd = q.shape
    W = ATT_BLOCK
    n = S // W
    qb = q.reshape(B, n, W, ATT_KV_HEADS, ATT_GROUP, hd)

    def with_prev(t):
        tb = t.reshape(B, n, W, ATT_KV_HEADS, hd)
        prev = jnp.pad(tb, ((0, 0), (1, 0), (0, 0), (0, 0), (0, 0)))[:, :-1]
        return jnp.concatenate([prev, tb], axis=2)

    kw, vw = with_prev(k), with_prev(v)
    s = jnp.einsum('bnqhgd,bnshd->bnhgqs', qb, kw).astype(jnp.float32) * (hd ** -0.5)
    blk = jnp.arange(n)[:, None, None] * W
    qpos = blk + jnp.arange(W)[None, :, None]
    kpos = blk - W + jnp.arange(2 * W)[None, None, :]
    delta = qpos - kpos
    mask = (delta >= 0) & (delta < WINDOW) & (kpos >= 0)
    s = jnp.where(mask[None, :, None, None], s, -jnp.inf)
    sink = sinks.astype(jnp.float32).reshape(1, 1, ATT_KV_HEADS, ATT_GROUP, 1, 1)
    m = jnp.maximum(jnp.max(s, axis=-1, keepdims=True), sink)
    p = jnp.exp(s - m)
    p = p / (jnp.sum(p, axis=-1, keepdims=True) + jnp.exp(sink - m))
    o = jnp.einsum('bnhgqs,bnshd->bnqhgd', p.astype(v.dtype), vw)
    return o.reshape(B, S, Hq * hd)


def setup_inputs(seed: int = 0) -> dict:
    key = jax.random.key(seed)
    ks = jax.random.split(key, 16)
    f32 = jnp.float32
    nrm = lambda k, shape, scale: jax.random.normal(k, shape, f32) * scale
    return {
        "x": nrm(ks[0], (BATCH, SEQ, D_MODEL), 1.0),
        "norm1": 1.0 + nrm(ks[1], (DEPTH, D_MODEL), 0.01),
        "w_in": nrm(ks[2], (DEPTH, D_MODEL, IN_COLS), D_MODEL ** -0.5),
        "lb_logits": nrm(ks[3], (DEPTH, HG_WIDTH), 0.5),
        "hg_norm": 1.0 + nrm(ks[4], (DEPTH, HG_WIDTH), 0.01),
        "attn_sinks": nrm(ks[5], (DEPTH, ATT_Q_HEADS), 0.5),
        "w_pa": nrm(ks[6], (DEPTH, HG_WIDTH, D_MODEL), HG_WIDTH ** -0.5),
        "w_pb": nrm(ks[7], (DEPTH, ATT_WIDTH, D_MODEL), ATT_WIDTH ** -0.5),
        "w_o": nrm(ks[8], (DEPTH, D_MODEL, D_MODEL), D_MODEL ** -0.5),
        "norm2": 1.0 + nrm(ks[9], (DEPTH, D_MODEL), 0.01),
        "w_gate": nrm(ks[10], (DEPTH, D_MODEL, FFN_HIDDEN), D_MODEL ** -0.5),
        "w_up": nrm(ks[11], (DEPTH, D_MODEL, FFN_HIDDEN), D_MODEL ** -0.5),
        "w_down": nrm(ks[12], (DEPTH, FFN_HIDDEN, D_MODEL), FFN_HIDDEN ** -0.5),
        "final_norm": 1.0 + nrm(ks[13], (D_MODEL,), 0.01),
    }


def reference(x, norm1, w_in, lb_logits, hg_norm, attn_sinks, w_pa, w_pb, w_o,
              norm2, w_gate, w_up, w_down, final_norm):
    B, S, _ = x.shape
    pos = jnp.arange(S)
    lb_p = jax.nn.softmax(lb_logits.astype(jnp.float32), axis=0)
    lb_all = jnp.cumsum(lb_p, axis=0) - lb_p[0:1]

    for l in range(DEPTH):
        h = rmsnorm(x, norm1[l])
        proj = h @ w_in[l]
        hq, hf, hi, hg, aq, ak, av, ga, gb = split_cols(proj)

        q = jax.nn.silu(hq).reshape(B, S, HG_HEADS, HG_DK).astype(jnp.float32)
        z = hf.reshape(B, S, HG_HEADS, HG_DK).astype(jnp.float32)
        lb = lb_all[l].reshape(HG_HEADS, HG_DK)
        f = lb + (1.0 - lb) * jax.nn.sigmoid(z)
        logf = jnp.log(jnp.maximum(f, MIN_F))
        kk = 1.0 - f
        vi = hi.reshape(B, S, HG_HEADS, HG_DV).astype(jnp.float32)
        o_hg = hgrn2_chunked(q, kk, vi, logf)
        o_hg = rmsnorm(o_hg, hg_norm[l].reshape(HG_HEADS, HG_DV)).astype(x.dtype)
        o_hg = o_hg.reshape(B, S, HG_WIDTH) * jax.nn.silu(hg)
        y_a = o_hg @ w_pa[l]

        qa = rope_partial(aq.reshape(B, S, ATT_Q_HEADS, ATT_HEAD_DIM), pos)
        ka = rope_partial(ak.reshape(B, S, ATT_KV_HEADS, ATT_HEAD_DIM), pos)
        va = av.reshape(B, S, ATT_KV_HEADS, ATT_HEAD_DIM)
        y_b = swa_with_sinks(qa, ka, va, attn_sinks[l]) @ w_pb[l]

        mix = jax.nn.sigmoid(ga) * y_a + jax.nn.sigmoid(gb) * y_b
        x = x + mix @ w_o[l]

        h2 = rmsnorm(x, norm2[l])
        x = x + (jax.nn.silu(h2 @ w_gate[l]) * (h2 @ w_up[l])) @ w_down[l]

    return rmsnorm(x, final_norm)
```

```python
import functools

import numpy as np
import jax
import jax.numpy as jnp
from jax import lax
from jax.experimental import pallas as pl
from jax.experimental.pallas import tpu as pltpu

D_MODEL = 1024
DEPTH = 4
HG_HEADS = 8
HG_DK = 128
HG_WIDTH = HG_HEADS * HG_DK
ATT_Q_HEADS = 16
ATT_KV_HEADS = 4
ATT_GROUP = ATT_Q_HEADS // ATT_KV_HEADS
ATT_HEAD_DIM = 64
ATT_WIDTH = ATT_Q_HEADS * ATT_HEAD_DIM
ATT_KV_WIDTH = ATT_KV_HEADS * ATT_HEAD_DIM
WINDOW = 128
ROPE_THETA = 500000.0
ROPE_DIM = ATT_HEAD_DIM // 4
ROPE_HALF = ROPE_DIM // 2
FFN_HIDDEN = ((8 * D_MODEL // 3 + 255) // 256) * 256
EPS = 1e-6
MIN_F = 1e-30
IN_COLS = 4 * HG_WIDTH + ATT_WIDTH + 2 * ATT_KV_WIDTH + 2 * D_MODEL

LANES = 128
VMEM_LIMIT = 56 * 1024 * 1024

CHUNK = 64
SUB = 16
EXP_CLAMP = 80.0

BF = jnp.bfloat16
F32 = jnp.float32

_NT = (((1,), (1,)), ((), ()))
_TN = (((0,), (0,)), ((), ()))


def _dot(a, b):
    return jnp.dot(a, b, preferred_element_type=F32)


def _dot_nt(a, b):
    return lax.dot_general(a, b, _NT, preferred_element_type=F32)


def _dot_tn(a, b):
    return lax.dot_general(a, b, _TN, preferred_element_type=F32)


def _sigmoid(x):
    return 1.0 / (1.0 + jnp.exp(-x))


def _in_proj_kernel(x_ref, g_ref, w_ref, lbl_ref, sel_ref, cos_ref, s1_ref, s2_ref,
                    q_ref, lf_ref, k_ref, v_ref, gs_ref, aq_ref, ak_ref, av_ref,
                    ga_ref, gb_ref):
    x = x_ref[...]
    h = x * lax.rsqrt(jnp.mean(x * x, axis=-1, keepdims=True) + EPS) * g_ref[...]
    h = h.astype(BF)

    def proj(col, width):
        return _dot(h, w_ref[:, col:col + width])

    lbl = lbl_ref[...]
    e = jnp.exp(lbl - jnp.max(lbl, axis=0, keepdims=True))
    p = e / jnp.sum(e, axis=0, keepdims=True)
    lb = jnp.sum(p * sel_ref[...], axis=0, keepdims=True)

    col = 0
    hq = proj(col, HG_WIDTH); col += HG_WIDTH
    q_ref[...] = (hq * _sigmoid(hq)).astype(BF)

    z = proj(col, HG_WIDTH); col += HG_WIDTH
    f = lb + (1.0 - lb) * _sigmoid(z)
    lf_ref[...] = jnp.log(jnp.maximum(f, MIN_F))
    k_ref[...] = (1.0 - f).astype(BF)

    v_ref[...] = proj(col, HG_WIDTH).astype(BF); col += HG_WIDTH

    hg = proj(col, HG_WIDTH); col += HG_WIDTH
    gs_ref[...] = (hg * _sigmoid(hg)).astype(BF)

    cos = cos_ref[...]
    s1 = s1_ref[...]
    s2 = s2_ref[...]

    def rope(t, scale):
        out = []
        for j in range(t.shape[1] // LANES):
            tj = t[:, j * LANES:(j + 1) * LANES]
            r = (tj * cos + pltpu.roll(tj, ROPE_HALF, axis=1) * s1
                 + pltpu.roll(tj, LANES - ROPE_HALF, axis=1) * s2)
            out.append(r * scale)
        return jnp.concatenate(out, axis=1)

    aq = proj(col, ATT_WIDTH); col += ATT_WIDTH
    aq_ref[...] = rope(aq, ATT_HEAD_DIM ** -0.5).astype(BF)
    ak = proj(col, ATT_KV_WIDTH); col += ATT_KV_WIDTH
    ak_ref[...] = rope(ak, 1.0).astype(BF)
    av_ref[...] = proj(col, ATT_KV_WIDTH).astype(BF); col += ATT_KV_WIDTH

    ga_ref[...] = _sigmoid(proj(col, D_MODEL)).astype(BF); col += D_MODEL
    gb_ref[...] = _sigmoid(proj(col, D_MODEL)).astype(BF); col += D_MODEL


def _in_proj(x2d, g, w, lb_logits, sel, cos_t, s1_t, s2_t, seq, tm=512):
    T = x2d.shape[0]
    n_pos_blocks = seq // tm
    row = lambda i: (i, 0)
    pos = lambda i: (i % n_pos_blocks, 0)
    const = lambda i: (0, 0)
    wide = lambda dt: jax.ShapeDtypeStruct((T, D_MODEL), dt)
    kvw = jax.ShapeDtypeStruct((T, ATT_KV_WIDTH), BF)
    out_shape = (wide(BF), wide(F32), wide(BF), wide(BF), wide(BF), wide(BF), kvw, kvw,
                 wide(BF), wide(BF))
    wide_spec = pl.BlockSpec((tm, D_MODEL), row)
    kv_spec = pl.BlockSpec((tm, ATT_KV_WIDTH), row)
    return pl.pallas_call(
        _in_proj_kernel,
        grid=(T // tm,),
        in_specs=[
            pl.BlockSpec((tm, D_MODEL), row),
            pl.BlockSpec((1, D_MODEL), const),
            pl.BlockSpec((D_MODEL, IN_COLS), const, pipeline_mode=pl.Buffered(1)),
            pl.BlockSpec((DEPTH, HG_WIDTH), const),
            pl.BlockSpec((DEPTH, 1), const),
            pl.BlockSpec((tm, LANES), pos),
            pl.BlockSpec((tm, LANES), pos),
            pl.BlockSpec((tm, LANES), pos),
        ],
        out_specs=(wide_spec,) * 6 + (kv_spec, kv_spec, wide_spec, wide_spec),
        out_shape=out_shape,
        compiler_params=pltpu.CompilerParams(
            dimension_semantics=("arbitrary",), vmem_limit_bytes=VMEM_LIMIT),
        name="in_proj",
    )(x2d, g, w, lb_logits, sel, cos_t, s1_t, s2_t)


def _decay_matrices():
    t = np.arange(CHUNK)[:, None]
    u = np.arange(CHUNK)[None, :]
    sub_start = (t // SUB) * SUB
    sub_end = sub_start + SUB - 1
    mid = sub_start + SUB // 2 - 1
    half = CHUNK // 2
    tri = (u <= t)
    dmid = tri.astype(np.int32) - (u <= mid)
    anti = (u > t)
    qa = (u >= sub_start) & (u <= t)
    ka = (u > t) & (u <= sub_end)
    qb = (u >= half) & (u <= t)
    kb = (u > t) & (u < half)
    mats = [tri, dmid, anti, qa, ka, qb, kb]
    return np.concatenate([m.astype(np.float32) for m in mats], axis=0)


def _score_masks():
    t = np.arange(CHUNK)[:, None]
    s = np.arange(CHUNK)[None, :]
    st, ss = t // SUB, s // SUB
    half = CHUNK // 2
    m_diag = (st == ss) & (s <= t)
    m_adj = ((st % 2) == 1) & (ss == st - 1)
    m_half = (t >= half) & (s < half)
    code = np.where(m_diag, 1, np.where(m_adj, 2, np.where(m_half, 3, 0)))
    return code.astype(np.int32)


def _hgrn2_kernel(q_ref, k_ref, v_ref, lf_ref, gs_ref, gn_ref, dm_ref, code_ref,
                  o_ref, st_ref, *, tb):
    @pl.when(pl.program_id(1) == 0)
    def _():
        st_ref[...] = jnp.zeros_like(st_ref)

    dm = dm_ref[...]
    code = code_ref[...]
    is_diag = code == 1
    is_adj = code == 2
    is_half = code == 3

    def chunk_body(c, carry):
        r0 = pl.multiple_of(c * CHUNK, CHUNK)
        rows = pl.ds(r0, CHUNK)
        for h in range(HG_HEADS):
            cols = slice(h * HG_DK, (h + 1) * HG_DK)
            lf = lf_ref[0, rows, cols]
            lf_hi = lf.astype(BF)
            lf_lo = (lf - lf_hi.astype(F32)).astype(BF)
            cs = _dot(dm, lf_hi) + _dot(dm, lf_lo)
            part = lambda i: cs[i * CHUNK:(i + 1) * CHUNK]
            b, amid, anti, qa, ka, qb, kb = [part(i) for i in range(7)]

            q = q_ref[0, rows, cols].astype(F32)
            k = k_ref[0, rows, cols].astype(F32)
            v = v_ref[0, rows, cols]

            q_in = (q * jnp.exp(b)).astype(BF)
            q_d = (q * jnp.exp(jnp.minimum(amid, EXP_CLAMP))).astype(BF)
            k_d = (k * jnp.exp(jnp.minimum(-amid, EXP_CLAMP))).astype(BF)
            q_a = (q * jnp.exp(qa)).astype(BF)
            k_a = (k * jnp.exp(ka)).astype(BF)
            q_b = (q * jnp.exp(qb)).astype(BF)
            k_b = (k * jnp.exp(kb)).astype(BF)
            k_s = (k * jnp.exp(anti)).astype(BF)

            s_d = _dot_nt(q_d, k_d)
            s_a = _dot_nt(q_a, k_a)
            s_b = _dot_nt(q_b, k_b)
            scores = jnp.where(is_diag, s_d,
                               jnp.where(is_adj, s_a, jnp.where(is_half, s_b, 0.0)))

            st = st_ref[h]
            o = _dot_nt(q_in, st.astype(BF)) + _dot(scores.astype(BF), v)
            st_ref[h] = st * jnp.exp(b[CHUNK - 1:CHUNK, :]) + _dot_tn(v, k_s)

            ms = jnp.mean(o * o, axis=-1, keepdims=True)
            o = o * lax.rsqrt(ms + EPS) * gn_ref[:, cols]
            o_ref[0, rows, cols] = (o * gs_ref[0, rows, cols].astype(F32)).astype(BF)
        return carry

    lax.fori_loop(0, tb // CHUNK, chunk_body, 0)


def _hgrn2(q, k, v, lf, gs, gn, dm, code, tb=256):
    B, S, _ = q.shape
    blk = lambda b, i: (b, i, 0)
    const = lambda b, i: (0, 0)
    tok = pl.BlockSpec((1, tb, HG_WIDTH), blk)
    return pl.pallas_call(
        functools.partial(_hgrn2_kernel, tb=tb),
        grid=(B, S // tb),
        in_specs=[tok, tok, tok, tok, tok,
                  pl.BlockSpec((1, HG_WIDTH), const),
                  pl.BlockSpec(dm.shape, const),
                  pl.BlockSpec(code.shape, const)],
        out_specs=tok,
        out_shape=jax.ShapeDtypeStruct((B, S, HG_WIDTH), BF),
        scratch_shapes=[pltpu.VMEM((HG_HEADS, HG_DK, HG_DK), F32)],
        compiler_params=pltpu.CompilerParams(
            dimension_semantics=("arbitrary", "arbitrary"), vmem_limit_bytes=VMEM_LIMIT),
        name="hgrn2",
    )(q, k, v, lf, gs, gn, dm, code)


def _swa_kernel(sink_ref, q_ref, kp_ref, kc_ref, vp_ref, vc_ref, o_ref):
    n = pl.program_id(1)
    W = WINDOW
    i = lax.broadcasted_iota(jnp.int32, (W, 2 * W), 0)
    j = lax.broadcasted_iota(jnp.int32, (W, 2 * W), 1)
    mask = (j > i) & (j <= i + W) & ((j >= W) | (n > 0))
    for hk in range(ATT_KV_HEADS):
        kcols = slice(hk * ATT_HEAD_DIM, (hk + 1) * ATT_HEAD_DIM)
        kk = jnp.concatenate([kp_ref[0, :, kcols], kc_ref[0, :, kcols]], axis=0)
        vv = jnp.concatenate([vp_ref[0, :, kcols], vc_ref[0, :, kcols]], axis=0)
        for g in range(ATT_GROUP):
            hq = hk * ATT_GROUP + g
            qcols = slice(hq * ATT_HEAD_DIM, (hq + 1) * ATT_HEAD_DIM)
            s = _dot_nt(q_ref[0, :, qcols], kk)
            s = jnp.where(mask, s, -jnp.inf)
            sink = sink_ref[hq]
            m = jnp.maximum(jnp.max(s, axis=-1, keepdims=True), sink)
            p = jnp.exp(s - m)
            denom = jnp.sum(p, axis=-1, keepdims=True) + jnp.exp(sink - m)
            o = _dot(p.astype(BF), vv) / denom
            o_ref[0, :, qcols] = o.astype(BF)


def _swa(sinks, aq, ak, av):
    B, S, _ = aq.shape
    W = WINDOW
    cur = lambda b, n: (b, n, 0)
    prev = lambda b, n: (b, jnp.maximum(n - 1, 0), 0)
    kv_cur = pl.BlockSpec((1, W, ATT_KV_WIDTH), cur)
    kv_prev = pl.BlockSpec((1, W, ATT_KV_WIDTH), prev)
    return pl.pallas_call(
        _swa_kernel,
        grid=(B, S // W),
        in_specs=[pl.BlockSpec(memory_space=pltpu.SMEM),
                  pl.BlockSpec((1, W, ATT_WIDTH), cur),
                  kv_prev, kv_cur, kv_prev, kv_cur],
        out_specs=pl.BlockSpec((1, W, ATT_WIDTH), cur),
        out_shape=jax.ShapeDtypeStruct((B, S, ATT_WIDTH), BF),
        compiler_params=pltpu.CompilerParams(
            dimension_semantics=("arbitrary", "arbitrary"), vmem_limit_bytes=VMEM_LIMIT),
        name="swa",
    )(sinks, aq, ak, ak, av, av)


def _merge_kernel(x_ref, oa_ref, ob_ref, ga_ref, gb_ref, wpa_ref, wpb_ref, wo_ref, o_ref):
    ya = _dot(oa_ref[...], wpa_ref[...])
    yb = _dot(ob_ref[...], wpb_ref[...])
    mix = ga_ref[...].astype(F32) * ya + gb_ref[...].astype(F32) * yb
    o_ref[...] = x_ref[...] + _dot(mix.astype(BF), wo_ref[...])


def _merge(x2d, oa, ob, ga, gb, wpa, wpb, wo, tm=512):
    T = x2d.shape[0]
    row = lambda i: (i, 0)
    const = lambda i: (0, 0)
    tok = pl.BlockSpec((tm, D_MODEL), row)
    wsp = pl.BlockSpec((D_MODEL, D_MODEL), const)
    return pl.pallas_call(
        _merge_kernel,
        grid=(T // tm,),
        in_specs=[tok, tok, tok, tok, tok, wsp, wsp, wsp],
        out_specs=tok,
        out_shape=jax.ShapeDtypeStruct((T, D_MODEL), F32),
        compiler_params=pltpu.CompilerParams(
            dimension_semantics=("arbitrary",), vmem_limit_bytes=VMEM_LIMIT),
        name="merge",
    )(x2d, oa, ob, ga, gb, wpa, wpb, wo)


FFN_COL_CHUNK = 704


def _ffn_kernel(x_ref, g_ref, wg_ref, wu_ref, wd_ref, fg_ref, o_ref, act_ref, *, final):
    x = x_ref[...]
    h = x * lax.rsqrt(jnp.mean(x * x, axis=-1, keepdims=True) + EPS) * g_ref[...]
    h = h.astype(BF)
    for c in range(FFN_HIDDEN // FFN_COL_CHUNK):
        cols = slice(c * FFN_COL_CHUNK, (c + 1) * FFN_COL_CHUNK)
        gate = _dot(h, wg_ref[:, cols])
        up = _dot(h, wu_ref[:, cols])
        act_ref[:, cols] = (gate * _sigmoid(gate) * up).astype(BF)
    y = x + _dot(act_ref[...], wd_ref[...])
    if final:
        y = y * lax.rsqrt(jnp.mean(y * y, axis=-1, keepdims=True) + EPS) * fg_ref[...]
    o_ref[...] = y


def _ffn(x2d, g, wg, wu, wd, fg, final, tm=512):
    T = x2d.shape[0]
    row = lambda i: (i, 0)
    const = lambda i: (0, 0)
    tok = pl.BlockSpec((tm, D_MODEL), row)
    vec = pl.BlockSpec((1, D_MODEL), const)
    single = pl.Buffered(1)
    return pl.pallas_call(
        functools.partial(_ffn_kernel, final=final),
        grid=(T // tm,),
        in_specs=[tok, vec,
                  pl.BlockSpec((D_MODEL, FFN_HIDDEN), const, pipeline_mode=single),
                  pl.BlockSpec((D_MODEL, FFN_HIDDEN), const, pipeline_mode=single),
                  pl.BlockSpec((FFN_HIDDEN, D_MODEL), const, pipeline_mode=single),
                  vec],
        out_specs=tok,
        out_shape=jax.ShapeDtypeStruct((T, D_MODEL), F32),
        scratch_shapes=[pltpu.VMEM((tm, FFN_HIDDEN), BF)],
        compiler_params=pltpu.CompilerParams(
            dimension_semantics=("arbitrary",), vmem_limit_bytes=VMEM_LIMIT),
        name="ffn_final" if final else "ffn",
    )(x2d, g, wg, wu, wd, fg)


def _rope_tables(seq):
    inv = ROPE_THETA ** (-jnp.arange(ROPE_HALF, dtype=F32) * 2.0 / ROPE_DIM)
    ang = jnp.arange(seq, dtype=F32)[:, None] * inv[None, :]
    cos, sin = jnp.cos(ang), jnp.sin(ang)
    ones = jnp.ones((seq, ATT_HEAD_DIM - ROPE_DIM), F32)
    zeros = jnp.zeros((seq, ATT_HEAD_DIM - ROPE_DIM), F32)
    zh = jnp.zeros_like(sin)
    cos_h = jnp.concatenate([cos, cos, ones], axis=1)
    s1_h = jnp.concatenate([zh, sin, zeros], axis=1)
    s2_h = jnp.concatenate([-sin, zh, zeros], axis=1)
    rep = LANES // ATT_HEAD_DIM
    return tuple(jnp.tile(t, (1, rep)) for t in (cos_h, s1_h, s2_h))


def kernel(x, norm1, w_in, lb_logits, hg_norm, attn_sinks, w_pa, w_pb, w_o,
           norm2, w_gate, w_up, w_down, final_norm):
    B, S, D = x.shape
    T = B * S
    cos_t, s1_t, s2_t = _rope_tables(S)
    dm = jnp.asarray(_decay_matrices(), BF)
    code = jnp.asarray(_score_masks())
    layer_ids = jnp.arange(DEPTH)[None, :, None]
    sel_all = ((layer_ids >= 1) & (layer_ids <= jnp.arange(DEPTH)[:, None, None])).astype(F32)

    x2d = x.reshape(T, D)
    for l in range(DEPTH):
        q, lf, k, v, gs, aq, ak, av, ga, gb = _in_proj(
            x2d, norm1[l][None, :], w_in[l].astype(BF), lb_logits, sel_all[l],
            cos_t, s1_t, s2_t, S)
        r3 = lambda a: a.reshape(B, S, a.shape[-1])
        o_hg = _hgrn2(r3(q), r3(k), r3(v), r3(lf), r3(gs), hg_norm[l][None, :], dm, code)
        o_at = _swa(attn_sinks[l], r3(aq), r3(ak), r3(av))
        x2d = _merge(x2d, o_hg.reshape(T, D), o_at.reshape(T, D), ga, gb,
                     w_pa[l].astype(BF), w_pb[l].astype(BF), w_o[l].astype(BF))
        x2d = _ffn(x2d, norm2[l][None, :], w_gate[l].astype(BF), w_up[l].astype(BF),
                   w_down[l].astype(BF), final_norm[None, :], final=(l == DEPTH - 1))
    return x2d.reshape(B, S, D)
```

```python
import functools

import numpy as np
import jax
import jax.numpy as jnp
from jax import lax
from jax.experimental import pallas as pl
from jax.experimental.pallas import tpu as pltpu

D_MODEL = 1024
DEPTH = 4
HG_HEADS = 8
HG_DK = 128
HG_WIDTH = HG_HEADS * HG_DK
ATT_Q_HEADS = 16
ATT_KV_HEADS = 4
ATT_GROUP = ATT_Q_HEADS // ATT_KV_HEADS
ATT_HEAD_DIM = 64
ATT_WIDTH = ATT_Q_HEADS * ATT_HEAD_DIM
ATT_KV_WIDTH = ATT_KV_HEADS * ATT_HEAD_DIM
WINDOW = 128
ROPE_THETA = 500000.0
ROPE_DIM = ATT_HEAD_DIM // 4
ROPE_HALF = ROPE_DIM // 2
FFN_HIDDEN = ((8 * D_MODEL // 3 + 255) // 256) * 256
EPS = 1e-6
MIN_F = 1e-30
IN_COLS = 4 * HG_WIDTH + ATT_WIDTH + 2 * ATT_KV_WIDTH + 2 * D_MODEL

LANES = 128
GROUP = 8
VMEM_LIMIT = 56 * 1024 * 1024

CHUNK = 64
SUB = 16
LOG2E = 1.4426950408889634
EXP_CLAMP = 80.0

BF = jnp.bfloat16
F32 = jnp.float32

_NT = (((1,), (1,)), ((), ()))
_TN = (((0,), (0,)), ((), ()))


def _dot(a, b):
    return jnp.dot(a, b, preferred_element_type=F32)


def _dot_nt(a, b):
    return lax.dot_general(a, b, _NT, preferred_element_type=F32)


def _dot_tn(a, b):
    return lax.dot_general(a, b, _TN, preferred_element_type=F32)


def _sigmoid(x):
    return 1.0 / (1.0 + jnp.exp(-x))


def _in_proj_kernel(x_ref, g_ref, w_ref, lbl_ref, sel_ref, cos_ref, s1_ref, s2_ref,
                    q_ref, lf_ref, k_ref, v_ref, gs_ref, aq_ref, ak_ref, av_ref,
                    ga_ref, gb_ref):
    x = x_ref[...]
    h = x * lax.rsqrt(jnp.mean(x * x, axis=-1, keepdims=True) + EPS) * g_ref[...]
    h = h.astype(BF)

    def proj(col, width):
        return _dot(h, w_ref[:, col:col + width])

    lbl = lbl_ref[...]
    e = jnp.exp(lbl - jnp.max(lbl, axis=0, keepdims=True))
    p = e / jnp.sum(e, axis=0, keepdims=True)
    lb = jnp.sum(p * sel_ref[...], axis=0, keepdims=True)

    col = 0
    hq = proj(col, HG_WIDTH); col += HG_WIDTH
    q_ref[...] = (hq * _sigmoid(hq)).astype(BF)

    z = proj(col, HG_WIDTH); col += HG_WIDTH
    f = lb + (1.0 - lb) * _sigmoid(z)
    lf_ref[...] = jnp.log(jnp.maximum(f, MIN_F))
    k_ref[...] = (1.0 - f).astype(BF)

    v_ref[...] = proj(col, HG_WIDTH).astype(BF); col += HG_WIDTH

    hg = proj(col, HG_WIDTH); col += HG_WIDTH
    gs_ref[...] = (hg * _sigmoid(hg)).astype(BF)

    cos = cos_ref[...]
    s1 = s1_ref[...]
    s2 = s2_ref[...]

    def rope(t, scale):
        out = []
        for j in range(t.shape[1] // LANES):
            tj = t[:, j * LANES:(j + 1) * LANES]
            r = (tj * cos + pltpu.roll(tj, ROPE_HALF, axis=1) * s1
                 + pltpu.roll(tj, LANES - ROPE_HALF, axis=1) * s2)
            out.append(r * scale)
        return jnp.concatenate(out, axis=1)

    aq = proj(col, ATT_WIDTH); col += ATT_WIDTH
    aq_ref[...] = rope(aq, ATT_HEAD_DIM ** -0.5 * LOG2E).astype(BF)
    ak = proj(col, ATT_KV_WIDTH); col += ATT_KV_WIDTH
    ak_ref[...] = rope(ak, 1.0).astype(BF)
    av_ref[...] = proj(col, ATT_KV_WIDTH).astype(BF); col += ATT_KV_WIDTH

    ga_ref[...] = _sigmoid(proj(col, D_MODEL)).astype(BF); col += D_MODEL
    gb_ref[...] = _sigmoid(proj(col, D_MODEL)).astype(BF); col += D_MODEL


def _in_proj(x2d, g, w, lb_logits, sel, cos_t, s1_t, s2_t, seq, tm=512):
    T = x2d.shape[0]
    n_pos_blocks = seq // tm
    row = lambda i: (i, 0)
    pos = lambda i: (i % n_pos_blocks, 0)
    const = lambda i: (0, 0)
    wide = lambda dt: jax.ShapeDtypeStruct((T, D_MODEL), dt)
    kvw = jax.ShapeDtypeStruct((T, ATT_KV_WIDTH), BF)
    out_shape = (wide(BF), wide(F32), wide(BF), wide(BF), wide(BF), wide(BF), kvw, kvw,
                 wide(BF), wide(BF))
    wide_spec = pl.BlockSpec((tm, D_MODEL), row)
    kv_spec = pl.BlockSpec((tm, ATT_KV_WIDTH), row)
    return pl.pallas_call(
        _in_proj_kernel,
        grid=(T // tm,),
        in_specs=[
            pl.BlockSpec((tm, D_MODEL), row),
            pl.BlockSpec((1, D_MODEL), const),
            pl.BlockSpec((D_MODEL, IN_COLS), const, pipeline_mode=pl.Buffered(1)),
            pl.BlockSpec((DEPTH, HG_WIDTH), const),
            pl.BlockSpec((DEPTH, 1), const),
            pl.BlockSpec((tm, LANES), pos),
            pl.BlockSpec((tm, LANES), pos),
            pl.BlockSpec((tm, LANES), pos),
        ],
        out_specs=(wide_spec,) * 6 + (kv_spec, kv_spec, wide_spec, wide_spec),
        out_shape=out_shape,
        compiler_params=pltpu.CompilerParams(
            dimension_semantics=("arbitrary",), vmem_limit_bytes=VMEM_LIMIT),
        name="in_proj",
    )(x2d, g, w, lb_logits, sel, cos_t, s1_t, s2_t)


def _score_masks():
    t = np.arange(CHUNK)[:, None]
    s = np.arange(CHUNK)[None, :]
    st, ss = t // SUB, s // SUB
    half = CHUNK // 2
    m_diag = (st == ss) & (s <= t)
    m_adj = ((st % 2) == 1) & (ss == st - 1)
    m_half = (t >= half) & (s < half)
    code = np.where(m_diag, 1, np.where(m_adj, 2, np.where(m_half, 3, 0)))
    return code.astype(np.int32)


def _decay_operands(lf, q, k, scan_masks):
    assert SUB == 2 * GROUP
    n_g = CHUNK // GROUP
    half_g = n_g // 2
    w = lf * LOG2E
    for s, m in zip((1, 2, 4), scan_masks):
        w = w + pltpu.roll(w, s, axis=0) * m
    wg = [w[g * GROUP:(g + 1) * GROUP] for g in range(n_g)]
    tot = [jnp.broadcast_to(x[GROUP - 1:GROUP], x.shape) for x in wg]
    rem = [t - x for t, x in zip(tot, wg)]

    def plus(x, y):
        return x if y is None else x + y

    def prefix(lo, hi):
        pre, acc = {}, None
        for g in range(lo, hi):
            pre[g] = acc
            acc = plus(tot[g], acc)
        return pre, acc

    def suffix(lo, hi):
        suf, acc = {}, None
        for g in reversed(range(lo, hi)):
            suf[g] = acc
            acc = plus(tot[g], acc)
        return suf

    pre_all, total = prefix(0, n_g)
    suf_all = suffix(0, n_g)
    pre_half, _ = prefix(half_g, n_g)
    suf_half = suffix(0, half_g)
    clamp = EXP_CLAMP * LOG2E

    e_in, e_s, e_qd, e_kd, e_qa, e_ka, e_qb, e_kb = ([] for _ in range(8))
    for g in range(n_g):
        first = g % 2 == 0
        odd_sub = (g // 2) % 2 == 1
        e_in.append(plus(wg[g], pre_all[g]))
        e_s.append(plus(rem[g], suf_all[g]))
        if first:
            e_qd.append(jnp.minimum(-rem[g], clamp))
            e_kd.append(rem[g])
        else:
            e_qd.append(wg[g])
            e_kd.append(jnp.minimum(-wg[g], clamp))
        e_qa.append((wg[g] if first else wg[g] + tot[g - 1]) if odd_sub else None)
        e_ka.append(None if odd_sub else (rem[g] + tot[g + 1] if first else rem[g]))
        e_qb.append(plus(wg[g], pre_half[g]) if g >= half_g else None)
        e_kb.append(plus(rem[g], suf_half[g]) if g < half_g else None)

    def scaled(x, exps):
        factor = jnp.concatenate(
            [jnp.ones((GROUP, x.shape[1]), F32) if e is None else jnp.exp2(e) for e in exps],
            axis=0)
        return x * factor.astype(BF)

    ops = (scaled(q, e_in), scaled(q, e_qd), scaled(k, e_kd), scaled(q, e_qa),
           scaled(k, e_ka), scaled(q, e_qb), scaled(k, e_kb), scaled(k, e_s))
    return ops, jnp.exp2(total)


def _hgrn2_kernel(q_ref, k_ref, v_ref, lf_ref, gs_ref, gn_ref, code_ref,
                  o_ref, st_ref, op_ref, dec_ref, p_ref, stb_ref, acc_ref, *, tb):
    @pl.when(pl.program_id(1) == 0)
    def _():
        st_ref[...] = jnp.zeros_like(st_ref)

    code = code_ref[...]
    is_diag = code == 1
    is_adj = code == 2
    is_half = code == 3
    sub = lax.broadcasted_iota(jnp.int32, (CHUNK, HG_DK), 0) % GROUP
    scan_masks = [(sub >= s).astype(F32) for s in (1, 2, 4)]
    heads = [slice(h * HG_DK, (h + 1) * HG_DK) for h in range(HG_HEADS)]

    for c in range(tb // CHUNK):
        rows = slice(c * CHUNK, (c + 1) * CHUNK)
        for cols in heads:
            ops, dec = _decay_operands(lf_ref[0, rows, cols], q_ref[0, rows, cols],
                                       k_ref[0, rows, cols], scan_masks)
            for i, op in enumerate(ops):
                op_ref[i, rows, cols] = op
            dec_ref[c, :, cols] = dec

        for h, cols in enumerate(heads):
            q_d, k_d, q_a, k_a, q_b, k_b = [op_ref[i, rows, cols] for i in range(1, 7)]
            s_d = _dot_nt(q_d, k_d)
            s_a = _dot_nt(q_a, k_a)
            s_b = _dot_nt(q_b, k_b)
            scores = jnp.where(is_diag, s_d,
                               jnp.where(is_adj, s_a, jnp.where(is_half, s_b, 0.0)))
            p_ref[h] = scores.astype(BF)

        for h, cols in enumerate(heads):
            st = st_ref[h]
            stb_ref[h] = st.astype(BF)
            st_ref[h] = st * dec_ref[c, 0:1, cols] + _dot_tn(v_ref[0, rows, cols],
                                                            op_ref[7, rows, cols])

        for h, cols in enumerate(heads):
            acc_ref[h] = (_dot_nt(op_ref[0, rows, cols], stb_ref[h])
                          + _dot(p_ref[h], v_ref[0, rows, cols]))

        for h, cols in enumerate(heads):
            o = acc_ref[h]
            ms = jnp.mean(o * o, axis=-1, keepdims=True)
            o = o * lax.rsqrt(ms + EPS) * gn_ref[:, cols]
            o_ref[0, rows, cols] = (o * gs_ref[0, rows, cols].astype(F32)).astype(BF)


def _hgrn2(q, k, v, lf, gs, gn, tb=256):
    B, S, _ = q.shape
    blk = lambda b, i: (b, i, 0)
    const = lambda b, i: (0, 0)
    tok = pl.BlockSpec((1, tb, HG_WIDTH), blk)
    code = jnp.asarray(_score_masks())
    return pl.pallas_call(
        functools.partial(_hgrn2_kernel, tb=tb),
        grid=(B, S // tb),
        in_specs=[tok, tok, tok, tok, tok,
                  pl.BlockSpec((1, HG_WIDTH), const),
                  pl.BlockSpec(code.shape, const)],
        out_specs=tok,
        out_shape=jax.ShapeDtypeStruct((B, S, HG_WIDTH), BF),
        scratch_shapes=[pltpu.VMEM((HG_HEADS, HG_DK, HG_DK), F32),
                        pltpu.VMEM((8, tb, HG_WIDTH), BF),
                        pltpu.VMEM((tb // CHUNK, GROUP, HG_WIDTH), F32),
                        pltpu.VMEM((HG_HEADS, CHUNK, CHUNK), BF),
                        pltpu.VMEM((HG_HEADS, HG_DK, HG_DK), BF),
                        pltpu.VMEM((HG_HEADS, CHUNK, HG_DK), F32)],
        compiler_params=pltpu.CompilerParams(
            dimension_semantics=("arbitrary", "arbitrary"), vmem_limit_bytes=VMEM_LIMIT),
        name="hgrn2",
    )(q, k, v, lf, gs, gn, code)


MASK_BIAS = -1e30
PAIR = LANES // ATT_HEAD_DIM


def _swa_bias():
    W = WINDOW
    i = np.arange(W)[:, None]
    j = np.arange(2 * W)[None, :]
    vis = (j > i) & (j <= i + W)
    first = vis & (j >= W)
    return np.where(np.stack([first, vis]), 0.0, MASK_BIAS).astype(np.float32)


def _swa_kernel(sink_ref, bias_ref, q_ref, kp_ref, kc_ref, vp_ref, vc_ref, o_ref,
                s_ref, p_ref, inv_ref):
    W = WINDOW
    lane = lax.broadcasted_iota(jnp.int32, (2 * W, LANES), 1)
    low = lane < ATT_HEAD_DIM
    low_q = lax.broadcasted_iota(jnp.int32, (W, LANES), 1) < ATT_HEAD_DIM

    def placed(t, hk):
        if hk % PAIR == 0:
            lo = jnp.where(low, t, jnp.zeros_like(t))
            return lo, pltpu.roll(lo, ATT_HEAD_DIM, axis=1)
        hi = jnp.where(low, jnp.zeros_like(t), t)
        return pltpu.roll(hi, ATT_HEAD_DIM, axis=1), hi

    def qk_stage(hk):
        tile = slice((hk // PAIR) * LANES, (hk // PAIR + 1) * LANES)
        k_lo, k_hi = placed(jnp.concatenate([kp_ref[0, :, tile], kc_ref[0, :, tile]], axis=0), hk)
        for g in range(ATT_GROUP):
            hq = hk * ATT_GROUP + g
            q2 = q_ref[0, :, (hq // PAIR) * LANES:(hq // PAIR + 1) * LANES]
            s_ref[hq] = _dot_nt(q2, k_lo if hq % PAIR == 0 else k_hi) + bias_ref[0]

    def softmax_stage(hk):
        for g in range(ATT_GROUP):
            hq = hk * ATT_GROUP + g
            s = s_ref[hq]
            sink = jnp.full((W, 1), sink_ref[hq], F32) * LOG2E
            m = jnp.maximum(jnp.max(s, axis=-1, keepdims=True), sink)
            p = jnp.exp2(s - m)
            p_ref[hq] = p.astype(BF)
            inv = 1.0 / (jnp.sum(p, axis=-1, keepdims=True) + jnp.exp2(sink - m))
            inv_ref[hq] = jnp.broadcast_to(inv, (W, LANES))

    def pv_stage(hk):
        tile = slice((hk // PAIR) * LANES, (hk // PAIR + 1) * LANES)
        v_lo, v_hi = placed(jnp.concatenate([vp_ref[0, :, tile], vc_ref[0, :, tile]], axis=0), hk)
        for pair in range(ATT_GROUP // PAIR):
            hq = hk * ATT_GROUP + pair * PAIR
            o2 = _dot(p_ref[hq], v_lo) + _dot(p_ref[hq + 1], v_hi)
            o2 = o2 * jnp.where(low_q, inv_ref[hq], inv_ref[hq + 1])
            o_ref[0, :, (hq // PAIR) * LANES:(hq // PAIR + 1) * LANES] = o2.astype(BF)

    n = ATT_KV_HEADS
    for step in range(n + 2):
        if step < n:
            qk_stage(step)
        if 1 <= step < n + 1:
            softmax_stage(step - 1)
        if step >= 2:
            pv_stage(step - 2)


def _swa(sinks, aq, ak, av):
    B, S, _ = aq.shape
    W = WINDOW
    cur = lambda b, n: (b, n, 0)
    prev = lambda b, n: (b, jnp.maximum(n - 1, 0), 0)
    kv_cur = pl.BlockSpec((1, W, ATT_KV_WIDTH), cur)
    kv_prev = pl.BlockSpec((1, W, ATT_KV_WIDTH), prev)
    bias = jnp.asarray(_swa_bias())
    return pl.pallas_call(
        _swa_kernel,
        grid=(B, S // W),
        in_specs=[pl.BlockSpec(memory_space=pltpu.SMEM),
                  pl.BlockSpec((1, W, 2 * W), lambda b, n: (jnp.minimum(n, 1), 0, 0)),
                  pl.BlockSpec((1, W, ATT_WIDTH), cur),
                  kv_prev, kv_cur, kv_prev, kv_cur],
        out_specs=pl.BlockSpec((1, W, ATT_WIDTH), cur),
        out_shape=jax.ShapeDtypeStruct((B, S, ATT_WIDTH), BF),
        scratch_shapes=[pltpu.VMEM((ATT_Q_HEADS, W, 2 * W), F32),
                        pltpu.VMEM((ATT_Q_HEADS, W, 2 * W), BF),
                        pltpu.VMEM((ATT_Q_HEADS, W, LANES), F32)],
        compiler_params=pltpu.CompilerParams(
            dimension_semantics=("arbitrary", "arbitrary"), vmem_limit_bytes=VMEM_LIMIT),
        name="swa",
    )(sinks, bias, aq, ak, ak, av, av)


def _merge_kernel(x_ref, oa_ref, ob_ref, ga_ref, gb_ref, wpa_ref, wpb_ref, wo_ref, o_ref):
    ya = _dot(oa_ref[...], wpa_ref[...])
    yb = _dot(ob_ref[...], wpb_ref[...])
    mix = ga_ref[...].astype(F32) * ya + gb_ref[...].astype(F32) * yb
    o_ref[...] = x_ref[...] + _dot(mix.astype(BF), wo_ref[...])


def _merge(x2d, oa, ob, ga, gb, wpa, wpb, wo, tm=512):
    T = x2d.shape[0]
    row = lambda i: (i, 0)
    const = lambda i: (0, 0)
    tok = pl.BlockSpec((tm, D_MODEL), row)
    wsp = pl.BlockSpec((D_MODEL, D_MODEL), const)
    return pl.pallas_call(
        _merge_kernel,
        grid=(T // tm,),
        in_specs=[tok, tok, tok, tok, tok, wsp, wsp, wsp],
        out_specs=tok,
        out_shape=jax.ShapeDtypeStruct((T, D_MODEL), F32),
        compiler_params=pltpu.CompilerParams(
            dimension_semantics=("arbitrary",), vmem_limit_bytes=VMEM_LIMIT),
        name="merge",
    )(x2d, oa, ob, ga, gb, wpa, wpb, wo)


FFN_COL_CHUNK = 704


def _ffn_kernel(x_ref, g_ref, wg_ref, wu_ref, wd_ref, fg_ref, o_ref, act_ref, *, final):
    x = x_ref[...]
    h = x * lax.rsqrt(jnp.mean(x * x, axis=-1, keepdims=True) + EPS) * g_ref[...]
    h = h.astype(BF)
    for c in range(FFN_HIDDEN // FFN_COL_CHUNK):
        cols = slice(c * FFN_COL_CHUNK, (c + 1) * FFN_COL_CHUNK)
        gate = _dot(h, wg_ref[:, cols])
        up = _dot(h, wu_ref[:, cols])
        act_ref[:, cols] = (gate * _sigmoid(gate) * up).astype(BF)
    y = x + _dot(act_ref[...], wd_ref[...])
    if final:
        y = y * lax.rsqrt(jnp.mean(y * y, axis=-1, keepdims=True) + EPS) * fg_ref[...]
    o_ref[...] = y


def _ffn(x2d, g, wg, wu, wd, fg, final, tm=512):
    T = x2d.shape[0]
    row = lambda i: (i, 0)
    const = lambda i: (0, 0)
    tok = pl.BlockSpec((tm, D_MODEL), row)
    vec = pl.BlockSpec((1, D_MODEL), const)
    single = pl.Buffered(1)
    return pl.pallas_call(
        functools.partial(_ffn_kernel, final=final),
        grid=(T // tm,),
        in_specs=[tok, vec,
                  pl.BlockSpec((D_MODEL, FFN_HIDDEN), const, pipeline_mode=single),
                  pl.BlockSpec((D_MODEL, FFN_HIDDEN), const, pipeline_mode=single),
                  pl.BlockSpec((FFN_HIDDEN, D_MODEL), const, pipeline_mode=single),
                  vec],
        out_specs=tok,
        out_shape=jax.ShapeDtypeStruct((T, D_MODEL), F32),
        scratch_shapes=[pltpu.VMEM((tm, FFN_HIDDEN), BF)],
        compiler_params=pltpu.CompilerParams(
            dimension_semantics=("arbitrary",), vmem_limit_bytes=VMEM_LIMIT),
        name="ffn_final" if final else "ffn",
    )(x2d, g, wg, wu, wd, fg)


def _rope_tables(seq):
    inv = ROPE_THETA ** (-jnp.arange(ROPE_HALF, dtype=F32) * 2.0 / ROPE_DIM)
    ang = jnp.arange(seq, dtype=F32)[:, None] * inv[None, :]
    cos, sin = jnp.cos(ang), jnp.sin(ang)
    ones = jnp.ones((seq, ATT_HEAD_DIM - ROPE_DIM), F32)
    zeros = jnp.zeros((seq, ATT_HEAD_DIM - ROPE_DIM), F32)
    zh = jnp.zeros_like(sin)
    cos_h = jnp.concatenate([cos, cos, ones], axis=1)
    s1_h = jnp.concatenate([zh, sin, zeros], axis=1)
    s2_h = jnp.concatenate([-sin, zh, zeros], axis=1)
    rep = LANES // ATT_HEAD_DIM
    return tuple(jnp.tile(t, (1, rep)) for t in (cos_h, s1_h, s2_h))


def kernel(x, norm1, w_in, lb_logits, hg_norm, attn_sinks, w_pa, w_pb, w_o,
           norm2, w_gate, w_up, w_down, final_norm):
    B, S, D = x.shape
    T = B * S
    cos_t, s1_t, s2_t = _rope_tables(S)
    layer_ids = jnp.arange(DEPTH)[None, :, None]
    sel_all = ((layer_ids >= 1) & (layer_ids <= jnp.arange(DEPTH)[:, None, None])).astype(F32)

    x2d = x.reshape(T, D)
    for l in range(DEPTH):
        q, lf, k, v, gs, aq, ak, av, ga, gb = _in_proj(
            x2d, norm1[l][None, :], w_in[l].astype(BF), lb_logits, sel_all[l],
            cos_t, s1_t, s2_t, S)
        r3 = lambda a: a.reshape(B, S, a.shape[-1])
        o_hg = _hgrn2(r3(q), r3(k), r3(v), r3(lf), r3(gs), hg_norm[l][None, :])
        o_at = _swa(attn_sinks[l], r3(aq), r3(ak), r3(av))
        x2d = _merge(x2d, o_hg.reshape(T, D), o_at.reshape(T, D), ga, gb,
                     w_pa[l].astype(BF), w_pb[l].astype(BF), w_o[l].astype(BF))
        x2d = _ffn(x2d, norm2[l][None, :], w_gate[l].astype(BF), w_up[l].astype(BF),
                   w_down[l].astype(BF), final_norm[None, :], final=(l == DEPTH - 1))
    return x2d.reshape(B, S, D)
```

```python
import functools

import numpy as np
import jax
import jax.numpy as jnp
from jax import lax
from jax.experimental import pallas as pl
from jax.experimental.pallas import tpu as pltpu

D_MODEL = 1024
DEPTH = 4
HG_HEADS = 8
HG_DK = 128
HG_WIDTH = HG_HEADS * HG_DK
ATT_Q_HEADS = 16
ATT_KV_HEADS = 4
ATT_GROUP = ATT_Q_HEADS // ATT_KV_HEADS
ATT_HEAD_DIM = 64
ATT_WIDTH = ATT_Q_HEADS * ATT_HEAD_DIM
ATT_KV_WIDTH = ATT_KV_HEADS * ATT_HEAD_DIM
WINDOW = 128
ROPE_THETA = 500000.0
ROPE_DIM = ATT_HEAD_DIM // 4
ROPE_HALF = ROPE_DIM // 2
FFN_HIDDEN = ((8 * D_MODEL // 3 + 255) // 256) * 256
EPS = 1e-6
MIN_F = 1e-30
IN_COLS = 4 * HG_WIDTH + ATT_WIDTH + 2 * ATT_KV_WIDTH + 2 * D_MODEL

LANES = 128
GROUP = 8
VMEM_LIMIT = 56 * 1024 * 1024

CHUNK = 64
SUB = 16
LOG2E = 1.4426950408889634
EXP_CLAMP = 80.0

BF = jnp.bfloat16
F32 = jnp.float32

_NT = (((1,), (1,)), ((), ()))
_TN = (((0,), (0,)), ((), ()))


def _dot(a, b):
    return jnp.dot(a, b, preferred_element_type=F32)


def _dot_nt(a, b):
    return lax.dot_general(a, b, _NT, preferred_element_type=F32)


def _dot_tn(a, b):
    return lax.dot_general(a, b, _TN, preferred_element_type=F32)


def _sigmoid(x):
    return 1.0 / (1.0 + jnp.exp(-x))


PK_Q, PK_K, PK_V, PK_GS, PK_AQ, PK_GA, PK_GB = range(7)
PK_AK = 7 * HG_WIDTH // ATT_KV_WIDTH
PK_AV = PK_AK + 1
PK_COLS = 7 * HG_WIDTH + 2 * ATT_KV_WIDTH


def _in_proj_kernel(x_ref, g_ref, w_ref, lbl_ref, sel_ref, rope_ref, pk_ref, lf_ref):
    def out(block, width=HG_WIDTH):
        return pk_ref.at[:, block * width:(block + 1) * width]

    q_ref, k_ref, v_ref, gs_ref = out(PK_Q), out(PK_K), out(PK_V), out(PK_GS)
    aq_ref, ga_ref, gb_ref = out(PK_AQ), out(PK_GA), out(PK_GB)
    ak_ref, av_ref = out(PK_AK, ATT_KV_WIDTH), out(PK_AV, ATT_KV_WIDTH)

    x = x_ref[...]
    h = x * lax.rsqrt(jnp.mean(x * x, axis=-1, keepdims=True) + EPS) * g_ref[...]
    h = h.astype(BF)

    def proj(col, width):
        return _dot(h, w_ref[:, col:col + width])

    lbl = lbl_ref[...]
    e = jnp.exp(lbl - jnp.max(lbl, axis=0, keepdims=True))
    p = e / jnp.sum(e, axis=0, keepdims=True)
    lb = jnp.sum(p * sel_ref[...], axis=0, keepdims=True)

    col = 0
    hq = proj(col, HG_WIDTH); col += HG_WIDTH
    q_ref[...] = (hq * _sigmoid(hq)).astype(BF)

    z = proj(col, HG_WIDTH); col += HG_WIDTH
    f = lb + (1.0 - lb) * _sigmoid(z)
    lf_ref[...] = jnp.log(jnp.maximum(f, MIN_F))
    k_ref[...] = (1.0 - f).astype(BF)

    v_ref[...] = proj(col, HG_WIDTH).astype(BF); col += HG_WIDTH

    hg = proj(col, HG_WIDTH); col += HG_WIDTH
    gs_ref[...] = (hg * _sigmoid(hg)).astype(BF)

    cos = rope_ref[:, 0:LANES]
    s1 = rope_ref[:, LANES:2 * LANES]
    s2 = rope_ref[:, 2 * LANES:3 * LANES]

    def rope(t, scale):
        out = []
        for j in range(t.shape[1] // LANES):
            tj = t[:, j * LANES:(j + 1) * LANES]
            r = (tj * cos + pltpu.roll(tj, ROPE_HALF, axis=1) * s1
                 + pltpu.roll(tj, LANES - ROPE_HALF, axis=1) * s2)
            out.append(r * scale)
        return jnp.concatenate(out, axis=1)

    aq = proj(col, ATT_WIDTH); col += ATT_WIDTH
    aq_ref[...] = rope(aq, ATT_HEAD_DIM ** -0.5 * LOG2E).astype(BF)
    ak = proj(col, ATT_KV_WIDTH); col += ATT_KV_WIDTH
    ak_ref[...] = rope(ak, 1.0).astype(BF)
    av_ref[...] = proj(col, ATT_KV_WIDTH).astype(BF); col += ATT_KV_WIDTH

    ga_ref[...] = _sigmoid(proj(col, D_MODEL)).astype(BF); col += D_MODEL
    gb_ref[...] = _sigmoid(proj(col, D_MODEL)).astype(BF); col += D_MODEL


def _in_proj(x2d, g, w, lb_logits, sel, rope_t, seq, tm=512):
    T = x2d.shape[0]
    n_pos_blocks = seq // tm
    row = lambda i: (i, 0)
    pos = lambda i: (i % n_pos_blocks, 0)
    const = lambda i: (0, 0)
    return pl.pallas_call(
        _in_proj_kernel,
        grid=(T // tm,),
        in_specs=[
            pl.BlockSpec((tm, D_MODEL), row),
            pl.BlockSpec((1, D_MODEL), const),
            pl.BlockSpec((D_MODEL, IN_COLS), const, pipeline_mode=pl.Buffered(1)),
            pl.BlockSpec((DEPTH, HG_WIDTH), const),
            pl.BlockSpec((DEPTH, 1), const),
            pl.BlockSpec((tm, 3 * LANES), pos),
        ],
        out_specs=(pl.BlockSpec((tm, PK_COLS), row), pl.BlockSpec((tm, HG_WIDTH), row)),
        out_shape=(jax.ShapeDtypeStruct((T, PK_COLS), BF),
                   jax.ShapeDtypeStruct((T, HG_WIDTH), F32)),
        compiler_params=pltpu.CompilerParams(
            dimension_semantics=("arbitrary",), vmem_limit_bytes=VMEM_LIMIT),
        name="in_proj",
    )(x2d, g, w, lb_logits, sel, rope_t)


def _score_masks():
    t = np.arange(CHUNK)[:, None]
    s = np.arange(CHUNK)[None, :]
    st, ss = t // SUB, s // SUB
    half = CHUNK // 2
    m_diag = (st == ss) & (s <= t)
    m_adj = ((st % 2) == 1) & (ss == st - 1)
    m_half = (t >= half) & (s < half)
    code = np.where(m_diag, 1, np.where(m_adj, 2, np.where(m_half, 3, 0)))
    return code.astype(np.int32)


def _decay_operands(lf, q, k, scan_masks):
    assert SUB == 2 * GROUP
    n_g = CHUNK // GROUP
    half_g = n_g // 2
    w = lf * LOG2E
    for s, m in zip((1, 2, 4), scan_masks):
        w = w + pltpu.roll(w, s, axis=0) * m
    wg = [w[g * GROUP:(g + 1) * GROUP] for g in range(n_g)]
    tot = [jnp.broadcast_to(x[GROUP - 1:GROUP], x.shape) for x in wg]
    rem = [t - x for t, x in zip(tot, wg)]

    def plus(x, y):
        return x if y is None else x + y

    def prefix(lo, hi):
        pre, acc = {}, None
        for g in range(lo, hi):
            pre[g] = acc
            acc = plus(tot[g], acc)
        return pre, acc

    def suffix(lo, hi):
        suf, acc = {}, None
        for g in reversed(range(lo, hi)):
            suf[g] = acc
            acc = plus(tot[g], acc)
        return suf

    pre_all, total = prefix(0, n_g)
    suf_all = suffix(0, n_g)
    pre_half, _ = prefix(half_g, n_g)
    suf_half = suffix(0, half_g)
    clamp = EXP_CLAMP * LOG2E

    e_in, e_s, e_qd, e_kd, e_qa, e_ka, e_qb, e_kb = ([] for _ in range(8))
    for g in range(n_g):
        first = g % 2 == 0
        odd_sub = (g // 2) % 2 == 1
        e_in.append(plus(wg[g], pre_all[g]))
        e_s.append(plus(rem[g], suf_all[g]))
        if first:
            e_qd.append(jnp.minimum(-rem[g], clamp))
            e_kd.append(rem[g])
        else:
            e_qd.append(wg[g])
            e_kd.append(jnp.minimum(-wg[g], clamp))
        e_qa.append((wg[g] if first else wg[g] + tot[g - 1]) if odd_sub else None)
        e_ka.append(None if odd_sub else (rem[g] + tot[g + 1] if first else rem[g]))
        e_qb.append(plus(wg[g], pre_half[g]) if g >= half_g else None)
        e_kb.append(plus(rem[g], suf_half[g]) if g < half_g else None)

    def scaled(x, exps):
        factor = jnp.concatenate(
            [jnp.ones((GROUP, x.shape[1]), F32) if e is None else jnp.exp2(e) for e in exps],
            axis=0)
        return x * factor.astype(BF)

    ops = (scaled(q, e_in), scaled(q, e_qd), scaled(k, e_kd), scaled(q, e_qa),
           scaled(k, e_ka), scaled(q, e_qb), scaled(k, e_kb), scaled(k, e_s))
    return ops, jnp.exp2(total)


def _hgrn2_kernel(q_ref, k_ref, v_ref, lf_ref, gs_ref, gn_ref, code_ref,
                  o_ref, st_ref, op_ref, dec_ref, p_ref, stb_ref, acc_ref, *, tb):
    @pl.when(pl.program_id(1) == 0)
    def _():
        st_ref[...] = jnp.zeros_like(st_ref)

    code = code_ref[...]
    is_diag = code == 1
    is_adj = code == 2
    is_half = code == 3
    sub = lax.broadcasted_iota(jnp.int32, (CHUNK, HG_DK), 0) % GROUP
    scan_masks = [(sub >= s).astype(F32) for s in (1, 2, 4)]
    heads = [slice(h * HG_DK, (h + 1) * HG_DK) for h in range(HG_HEADS)]

    for c in range(tb // CHUNK):
        rows = slice(c * CHUNK, (c + 1) * CHUNK)
        for cols in heads:
            ops, dec = _decay_operands(lf_ref[rows, cols], q_ref[rows, cols],
                                       k_ref[rows, cols], scan_masks)
            for i, op in enumerate(ops):
                op_ref[i, rows, cols] = op
            dec_ref[c, :, cols] = dec

        for h, cols in enumerate(heads):
            q_d, k_d, q_a, k_a, q_b, k_b = [op_ref[i, rows, cols] for i in range(1, 7)]
            s_d = _dot_nt(q_d, k_d)
            s_a = _dot_nt(q_a, k_a)
            s_b = _dot_nt(q_b, k_b)
            scores = jnp.where(is_diag, s_d,
                               jnp.where(is_adj, s_a, jnp.where(is_half, s_b, 0.0)))
            p_ref[h] = scores.astype(BF)

        for h, cols in enumerate(heads):
            st = st_ref[h]
            stb_ref[h] = st.astype(BF)
            st_ref[h] = st * dec_ref[c, 0:1, cols] + _dot_tn(v_ref[rows, cols],
                                                            op_ref[7, rows, cols])

        for h, cols in enumerate(heads):
            acc_ref[h] = (_dot_nt(op_ref[0, rows, cols], stb_ref[h])
                          + _dot(p_ref[h], v_ref[rows, cols]))

        for h, cols in enumerate(heads):
            o = acc_ref[h]
            ms = jnp.mean(o * o, axis=-1, keepdims=True)
            o = o * lax.rsqrt(ms + EPS) * gn_ref[:, cols]
            o_ref[rows, cols] = (o * gs_ref[rows, cols].astype(F32)).astype(BF)


def _hgrn2(pk, lf, gn, batch, tb=512):
    T = lf.shape[0]
    nblk = T // batch // tb
    const = lambda b, i: (0, 0)
    tok = lambda block: pl.BlockSpec((tb, HG_WIDTH), lambda b, i: (b * nblk + i, block))
    code = jnp.asarray(_score_masks())
    return pl.pallas_call(
        functools.partial(_hgrn2_kernel, tb=tb),
        grid=(batch, nblk),
        in_specs=[tok(PK_Q), tok(PK_K), tok(PK_V), tok(0), tok(PK_GS),
                  pl.BlockSpec((1, HG_WIDTH), const),
                  pl.BlockSpec(code.shape, const)],
        out_specs=tok(0),
        out_shape=jax.ShapeDtypeStruct((T, HG_WIDTH), BF),
        scratch_shapes=[pltpu.VMEM((HG_HEADS, HG_DK, HG_DK), F32),
                        pltpu.VMEM((8, tb, HG_WIDTH), BF),
                        pltpu.VMEM((tb // CHUNK, GROUP, HG_WIDTH), F32),
                        pltpu.VMEM((HG_HEADS, CHUNK, CHUNK), BF),
                        pltpu.VMEM((HG_HEADS, HG_DK, HG_DK), BF),
                        pltpu.VMEM((HG_HEADS, CHUNK, HG_DK), F32)],
        compiler_params=pltpu.CompilerParams(
            dimension_semantics=("arbitrary", "arbitrary"), vmem_limit_bytes=VMEM_LIMIT),
        name="hgrn2",
    )(pk, pk, pk, lf, pk, gn, code)


MASK_BIAS = -1e30
PAIR = LANES // ATT_HEAD_DIM


def _swa_bias():
    W = WINDOW
    i = np.arange(W)[:, None]
    j = np.arange(2 * W)[None, :]
    vis = (j > i) & (j <= i + W)
    first = vis & (j >= W)
    return np.where(np.stack([first, vis]), 0.0, MASK_BIAS).astype(np.float32)


def _swa_kernel(sink_ref, bias_ref, q_ref, kp_ref, kc_ref, vp_ref, vc_ref, o_ref,
                s_ref, p_ref, inv_ref):
    W = WINDOW
    lane = lax.broadcasted_iota(jnp.int32, (2 * W, LANES), 1)
    low = lane < ATT_HEAD_DIM
    low_q = lax.broadcasted_iota(jnp.int32, (W, LANES), 1) < ATT_HEAD_DIM

    def placed(t, hk):
        if hk % PAIR == 0:
            lo = jnp.where(low, t, jnp.zeros_like(t))
            return lo, pltpu.roll(lo, ATT_HEAD_DIM, axis=1)
        hi = jnp.where(low, jnp.zeros_like(t), t)
        return pltpu.roll(hi, ATT_HEAD_DIM, axis=1), hi

    def qk_stage(hk):
        tile = slice((hk // PAIR) * LANES, (hk // PAIR + 1) * LANES)
        k_lo, k_hi = placed(jnp.concatenate([kp_ref[:, tile], kc_ref[:, tile]], axis=0), hk)
        for g in range(ATT_GROUP):
            hq = hk * ATT_GROUP + g
            q2 = q_ref[:, (hq // PAIR) * LANES:(hq // PAIR + 1) * LANES]
            s_ref[hq] = _dot_nt(q2, k_lo if hq % PAIR == 0 else k_hi) + bias_ref[0]

    def softmax_stage(hk):
        for g in range(ATT_GROUP):
            hq = hk * ATT_GROUP + g
            s = s_ref[hq]
            sink = jnp.full((W, 1), sink_ref[hq], F32) * LOG2E
            m = jnp.maximum(jnp.max(s, axis=-1, keepdims=True), sink)
            p = jnp.exp2(s - m)
            p_ref[hq] = p.astype(BF)
            inv = 1.0 / (jnp.sum(p, axis=-1, keepdims=True) + jnp.exp2(sink - m))
            inv_ref[hq] = jnp.broadcast_to(inv, (W, LANES))

    def pv_stage(hk):
        tile = slice((hk // PAIR) * LANES, (hk // PAIR + 1) * LANES)
        v_lo, v_hi = placed(jnp.concatenate([vp_ref[:, tile], vc_ref[:, tile]], axis=0), hk)
        for pair in range(ATT_GROUP // PAIR):
            hq = hk * ATT_GROUP + pair * PAIR
            o2 = _dot(p_ref[hq], v_lo) + _dot(p_ref[hq + 1], v_hi)
            o2 = o2 * jnp.where(low_q, inv_ref[hq], inv_ref[hq + 1])
            o_ref[:, (hq // PAIR) * LANES:(hq // PAIR + 1) * LANES] = o2.astype(BF)

    n = ATT_KV_HEADS
    for step in range(n + 2):
        if step < n:
            qk_stage(step)
        if 1 <= step < n + 1:
            softmax_stage(step - 1)
        if step >= 2:
            pv_stage(step - 2)


def _swa(sinks, pk, batch):
    T = pk.shape[0]
    W = WINDOW
    nblk = T // batch // W
    kv_cur = lambda block: pl.BlockSpec((W, ATT_KV_WIDTH), lambda b, n: (b * nblk + n, block))
    kv_prev = lambda block: pl.BlockSpec(
        (W, ATT_KV_WIDTH), lambda b, n: (b * nblk + jnp.maximum(n - 1, 0), block))
    bias = jnp.asarray(_swa_bias())
    return pl.pallas_call(
        _swa_kernel,
        grid=(batch, nblk),
        in_specs=[pl.BlockSpec(memory_space=pltpu.SMEM),
                  pl.BlockSpec((1, W, 2 * W), lambda b, n: (jnp.minimum(n, 1), 0, 0)),
                  pl.BlockSpec((W, ATT_WIDTH), lambda b, n: (b * nblk + n, PK_AQ)),
                  kv_prev(PK_AK), kv_cur(PK_AK), kv_prev(PK_AV), kv_cur(PK_AV)],
        out_specs=pl.BlockSpec((W, ATT_WIDTH), lambda b, n: (b * nblk + n, 0)),
        out_shape=jax.ShapeDtypeStruct((T, ATT_WIDTH), BF),
        scratch_shapes=[pltpu.VMEM((ATT_Q_HEADS, W, 2 * W), F32),
                        pltpu.VMEM((ATT_Q_HEADS, W, 2 * W), BF),
                        pltpu.VMEM((ATT_Q_HEADS, W, LANES), F32)],
        compiler_params=pltpu.CompilerParams(
            dimension_semantics=("arbitrary", "arbitrary"), vmem_limit_bytes=VMEM_LIMIT),
        name="swa",
    )(sinks, bias, pk, pk, pk, pk, pk)


def _merge_kernel(x_ref, oa_ref, ob_ref, ga_ref, gb_ref, wpa_ref, wpb_ref, wo_ref, o_ref):
    ya = _dot(oa_ref[...], wpa_ref[...])
    yb = _dot(ob_ref[...], wpb_ref[...])
    mix = ga_ref[...].astype(F32) * ya + gb_ref[...].astype(F32) * yb
    o_ref[...] = x_ref[...] + _dot(mix.astype(BF), wo_ref[...])


def _merge(x2d, oa, ob, pk, wpa, wpb, wo, tm=1024):
    T = x2d.shape[0]
    const = lambda i: (0, 0)
    tok = lambda block: pl.BlockSpec((tm, D_MODEL), lambda i: (i, block))
    wsp = pl.BlockSpec((D_MODEL, D_MODEL), const)
    return pl.pallas_call(
        _merge_kernel,
        grid=(T // tm,),
        in_specs=[tok(0), tok(0), tok(0), tok(PK_GA), tok(PK_GB), wsp, wsp, wsp],
        out_specs=tok(0),
        out_shape=jax.ShapeDtypeStruct((T, D_MODEL), F32),
        compiler_params=pltpu.CompilerParams(
            dimension_semantics=("arbitrary",), vmem_limit_bytes=VMEM_LIMIT),
        name="merge",
    )(x2d, oa, ob, pk, pk, wpa, wpb, wo)


FFN_COL_CHUNK = 704


def _ffn_kernel(x_ref, g_ref, wg_ref, wu_ref, wd_ref, fg_ref, o_ref, act_ref, *, final):
    x = x_ref[...]
    h = x * lax.rsqrt(jnp.mean(x * x, axis=-1, keepdims=True) + EPS) * g_ref[...]
    h = h.astype(BF)
    for c in range(FFN_HIDDEN // FFN_COL_CHUNK):
        cols = slice(c * FFN_COL_CHUNK, (c + 1) * FFN_COL_CHUNK)
        gate = _dot(h, wg_ref[:, cols])
        up = _dot(h, wu_ref[:, cols])
        act_ref[:, cols] = (gate * _sigmoid(gate) * up).astype(BF)
    y = x + _dot(act_ref[...], wd_ref[...])
    if final:
        y = y * lax.rsqrt(jnp.mean(y * y, axis=-1, keepdims=True) + EPS) * fg_ref[...]
    o_ref[...] = y


def _ffn(x2d, g, wg, wu, wd, fg, final, tm=512):
    T = x2d.shape[0]
    row = lambda i: (i, 0)
    const = lambda i: (0, 0)
    tok = pl.BlockSpec((tm, D_MODEL), row)
    vec = pl.BlockSpec((1, D_MODEL), const)
    single = pl.Buffered(1)
    return pl.pallas_call(
        functools.partial(_ffn_kernel, final=final),
        grid=(T // tm,),
        in_specs=[tok, vec,
                  pl.BlockSpec((D_MODEL, FFN_HIDDEN), const, pipeline_mode=single),
                  pl.BlockSpec((D_MODEL, FFN_HIDDEN), const, pipeline_mode=single),
                  pl.BlockSpec((FFN_HIDDEN, D_MODEL), const, pipeline_mode=single),
                  vec],
        out_specs=tok,
        out_shape=jax.ShapeDtypeStruct((T, D_MODEL), F32),
        scratch_shapes=[pltpu.VMEM((tm, FFN_HIDDEN), BF)],
        compiler_params=pltpu.CompilerParams(
            dimension_semantics=("arbitrary",), vmem_limit_bytes=VMEM_LIMIT),
        name="ffn_final" if final else "ffn",
    )(x2d, g, wg, wu, wd, fg)


def _rope_tables(seq):
    inv = ROPE_THETA ** (-jnp.arange(ROPE_HALF, dtype=F32) * 2.0 / ROPE_DIM)
    ang = jnp.arange(seq, dtype=F32)[:, None] * inv[None, :]
    cos, sin = jnp.cos(ang), jnp.sin(ang)
    ones = jnp.ones((seq, ATT_HEAD_DIM - ROPE_DIM), F32)
    zeros = jnp.zeros((seq, ATT_HEAD_DIM - ROPE_DIM), F32)
    zh = jnp.zeros_like(sin)
    cos_h = jnp.concatenate([cos, cos, ones], axis=1)
    s1_h = jnp.concatenate([zh, sin, zeros], axis=1)
    s2_h = jnp.concatenate([-sin, zh, zeros], axis=1)
    rep = LANES // ATT_HEAD_DIM
    return jnp.concatenate([jnp.tile(t, (1, rep)) for t in (cos_h, s1_h, s2_h)], axis=1)


def kernel(x, norm1, w_in, lb_logits, hg_norm, attn_sinks, w_pa, w_pb, w_o,
           norm2, w_gate, w_up, w_down, final_norm):
    B, S, D = x.shape
    T = B * S
    rope_t = _rope_tables(S)
    layer_ids = jnp.arange(DEPTH)[None, :, None]
    sel_all = ((layer_ids >= 1) & (layer_ids <= jnp.arange(DEPTH)[:, None, None])).astype(F32)

    x2d = x.reshape(T, D)
    for l in range(DEPTH):
        pk, lf = _in_proj(x2d, norm1[l][None, :], w_in[l].astype(BF), lb_logits, sel_all[l],
                          rope_t, S)
        o_hg = _hgrn2(pk, lf, hg_norm[l][None, :], B)
        o_at = _swa(attn_sinks[l], pk, B)
        x2d = _merge(x2d, o_hg, o_at, pk,
                     w_pa[l].astype(BF), w_pb[l].astype(BF), w_o[l].astype(BF))
        x2d = _ffn(x2d, norm2[l][None, :], w_gate[l].astype(BF), w_up[l].astype(BF),
                   w_down[l].astype(BF), final_norm[None, :], final=(l == DEPTH - 1))
    return x2d.reshape(B, S, D)
```

```python
import functools

import numpy as np
import jax
import jax.numpy as jnp
from jax import lax
from jax.experimental import pallas as pl
from jax.experimental.pallas import tpu as pltpu

D_MODEL = 1024
DEPTH = 4
HG_HEADS = 8
HG_DK = 128
HG_WIDTH = HG_HEADS * HG_DK
ATT_Q_HEADS = 16
ATT_KV_HEADS = 4
ATT_GROUP = ATT_Q_HEADS // ATT_KV_HEADS
ATT_HEAD_DIM = 64
ATT_WIDTH = ATT_Q_HEADS * ATT_HEAD_DIM
ATT_KV_WIDTH = ATT_KV_HEADS * ATT_HEAD_DIM
WINDOW = 128
ROPE_THETA = 500000.0
ROPE_DIM = ATT_HEAD_DIM // 4
ROPE_HALF = ROPE_DIM // 2
FFN_HIDDEN = ((8 * D_MODEL // 3 + 255) // 256) * 256
EPS = 1e-6
MIN_F = 1e-30
IN_COLS = 4 * HG_WIDTH + ATT_WIDTH + 2 * ATT_KV_WIDTH + 2 * D_MODEL

LANES = 128
GROUP = 8
VMEM_LIMIT = 56 * 1024 * 1024

CHUNK = 64
SUB = 16
LOG2E = 1.4426950408889634
EXP_CLAMP = 80.0

BF = jnp.bfloat16
F32 = jnp.float32

_NT = (((1,), (1,)), ((), ()))
_TN = (((0,), (0,)), ((), ()))


def _dot(a, b):
    return jnp.dot(a, b, preferred_element_type=F32)


def _dot_nt(a, b):
    return lax.dot_general(a, b, _NT, preferred_element_type=F32)


def _dot_tn(a, b):
    return lax.dot_general(a, b, _TN, preferred_element_type=F32)


def _sigmoid(x):
    return 1.0 / (1.0 + jnp.exp(-x))


PK_Q, PK_K, PK_V, PK_GS, PK_AQ, PK_GA, PK_GB = range(7)
PK_AK = 7 * HG_WIDTH // ATT_KV_WIDTH
PK_AV = PK_AK + 1
PK_COLS = 7 * HG_WIDTH + 2 * ATT_KV_WIDTH


def _in_proj_kernel(x_ref, g_ref, w_ref, lbl_ref, sel_ref, rope_ref, pk_ref, lf_ref):
    def out(block, width=HG_WIDTH):
        return pk_ref.at[:, block * width:(block + 1) * width]

    q_ref, k_ref, v_ref, gs_ref = out(PK_Q), out(PK_K), out(PK_V), out(PK_GS)
    aq_ref, ga_ref, gb_ref = out(PK_AQ), out(PK_GA), out(PK_GB)
    ak_ref, av_ref = out(PK_AK, ATT_KV_WIDTH), out(PK_AV, ATT_KV_WIDTH)

    x = x_ref[...]
    h = x * lax.rsqrt(jnp.mean(x * x, axis=-1, keepdims=True) + EPS) * g_ref[...]
    h = h.astype(BF)

    def proj(col, width):
        return _dot(h, w_ref[:, col:col + width])

    lbl = lbl_ref[...]
    e = jnp.exp(lbl - jnp.max(lbl, axis=0, keepdims=True))
    p = e / jnp.sum(e, axis=0, keepdims=True)
    lb = jnp.sum(p * sel_ref[...], axis=0, keepdims=True)

    col = 0
    hq = proj(col, HG_WIDTH); col += HG_WIDTH
    q_ref[...] = (hq * _sigmoid(hq)).astype(BF)

    z = proj(col, HG_WIDTH); col += HG_WIDTH
    f = lb + (1.0 - lb) * _sigmoid(z)
    lf_ref[...] = jnp.log(jnp.maximum(f, MIN_F))
    k_ref[...] = (1.0 - f).astype(BF)

    v_ref[...] = proj(col, HG_WIDTH).astype(BF); col += HG_WIDTH

    hg = proj(col, HG_WIDTH); col += HG_WIDTH
    gs_ref[...] = (hg * _sigmoid(hg)).astype(BF)

    cos = rope_ref[:, 0:LANES]
    s1 = rope_ref[:, LANES:2 * LANES]
    s2 = rope_ref[:, 2 * LANES:3 * LANES]

    def rope(t, scale):
        out = []
        for j in range(t.shape[1] // LANES):
            tj = t[:, j * LANES:(j + 1) * LANES]
            r = (tj * cos + pltpu.roll(tj, ROPE_HALF, axis=1) * s1
                 + pltpu.roll(tj, LANES - ROPE_HALF, axis=1) * s2)
            out.append(r * scale)
        return jnp.concatenate(out, axis=1)

    aq = proj(col, ATT_WIDTH); col += ATT_WIDTH
    aq_ref[...] = rope(aq, ATT_HEAD_DIM ** -0.5 * LOG2E).astype(BF)
    ak = proj(col, ATT_KV_WIDTH); col += ATT_KV_WIDTH
    ak_ref[...] = rope(ak, 1.0).astype(BF)
    av_ref[...] = proj(col, ATT_KV_WIDTH).astype(BF); col += ATT_KV_WIDTH

    ga_ref[...] = _sigmoid(proj(col, D_MODEL)).astype(BF); col += D_MODEL
    gb_ref[...] = _sigmoid(proj(col, D_MODEL)).astype(BF); col += D_MODEL


def _in_proj(x2d, g, w, lb_logits, sel, rope_t, seq, tm=256):
    T = x2d.shape[0]
    n_pos_blocks = seq // tm
    row = lambda i: (i, 0)
    pos = lambda i: (i % n_pos_blocks, 0)
    const = lambda i: (0, 0)
    return pl.pallas_call(
        _in_proj_kernel,
        grid=(T // tm,),
        in_specs=[
            pl.BlockSpec((tm, D_MODEL), row),
            pl.BlockSpec((1, D_MODEL), const),
            pl.BlockSpec((D_MODEL, IN_COLS), const, pipeline_mode=pl.Buffered(1)),
            pl.BlockSpec((DEPTH, HG_WIDTH), const),
            pl.BlockSpec((DEPTH, 1), const),
            pl.BlockSpec((tm, 3 * LANES), pos),
        ],
        out_specs=(pl.BlockSpec((tm, PK_COLS), row), pl.BlockSpec((tm, HG_WIDTH), row)),
        out_shape=(jax.ShapeDtypeStruct((T, PK_COLS), BF),
                   jax.ShapeDtypeStruct((T, HG_WIDTH), F32)),
        compiler_params=pltpu.CompilerParams(
            dimension_semantics=("arbitrary",), vmem_limit_bytes=VMEM_LIMIT),
        name="in_proj",
    )(x2d, g, w, lb_logits, sel, rope_t)


def _score_masks():
    t = np.arange(CHUNK)[:, None]
    s = np.arange(CHUNK)[None, :]
    st, ss = t // SUB, s // SUB
    half = CHUNK // 2
    m_diag = (st == ss) & (s <= t)
    m_adj = ((st % 2) == 1) & (ss == st - 1)
    m_half = (t >= half) & (s < half)
    code = np.where(m_diag, 1, np.where(m_adj, 2, np.where(m_half, 3, 0)))
    return code.astype(np.int32)


def _decay_operands(lf, q, k, scan_masks):
    assert SUB == 2 * GROUP
    n_g = CHUNK // GROUP
    half_g = n_g // 2
    w = lf * LOG2E
    for s, m in zip((1, 2, 4), scan_masks):
        w = w + pltpu.roll(w, s, axis=0) * m
    wg = [w[g * GROUP:(g + 1) * GROUP] for g in range(n_g)]
    tot = [jnp.broadcast_to(x[GROUP - 1:GROUP], x.shape) for x in wg]
    rem = [t - x for t, x in zip(tot, wg)]

    def plus(x, y):
        return x if y is None else x + y

    def prefix(lo, hi):
        pre, acc = {}, None
        for g in range(lo, hi):
            pre[g] = acc
            acc = plus(tot[g], acc)
        return pre, acc

    def suffix(lo, hi):
        suf, acc = {}, None
        for g in reversed(range(lo, hi)):
            suf[g] = acc
            acc = plus(tot[g], acc)
        return suf

    pre_all, total = prefix(0, n_g)
    suf_all = suffix(0, n_g)
    pre_half, _ = prefix(half_g, n_g)
    suf_half = suffix(0, half_g)
    clamp = EXP_CLAMP * LOG2E

    e_in, e_s, e_qd, e_kd, e_qa, e_ka, e_qb, e_kb = ([] for _ in range(8))
    for g in range(n_g):
        first = g % 2 == 0
        odd_sub = (g // 2) % 2 == 1
        e_in.append(plus(wg[g], pre_all[g]))
        e_s.append(plus(rem[g], suf_all[g]))
        if first:
            e_qd.append(jnp.minimum(-rem[g], clamp))
            e_kd.append(rem[g])
        else:
            e_qd.append(wg[g])
            e_kd.append(jnp.minimum(-wg[g], clamp))
        e_qa.append((wg[g] if first else wg[g] + tot[g - 1]) if odd_sub else None)
        e_ka.append(None if odd_sub else (rem[g] + tot[g + 1] if first else rem[g]))
        e_qb.append(plus(wg[g], pre_half[g]) if g >= half_g else None)
        e_kb.append(plus(rem[g], suf_half[g]) if g < half_g else None)

    def scaled(x, exps):
        factor = jnp.concatenate(
            [jnp.ones((GROUP, x.shape[1]), F32) if e is None else jnp.exp2(e) for e in exps],
            axis=0)
        return x * factor.astype(BF)

    ops = (scaled(q, e_in), scaled(q, e_qd), scaled(k, e_kd), scaled(q, e_qa),
           scaled(k, e_ka), scaled(q, e_qb), scaled(k, e_kb), scaled(k, e_s))
    return ops, jnp.exp2(total)


def _hgrn2_kernel(q_ref, k_ref, v_ref, lf_ref, gs_ref, gn_ref, code_ref,
                  o_ref, st_ref, op_ref, dec_ref, p_ref, stb_ref, acc_ref, *, tb):
    @pl.when(pl.program_id(1) == 0)
    def _():
        st_ref[...] = jnp.zeros_like(st_ref)

    code = code_ref[...]
    is_diag = code == 1
    is_adj = code == 2
    is_half = code == 3
    sub = lax.broadcasted_iota(jnp.int32, (CHUNK, HG_DK), 0) % GROUP
    scan_masks = [(sub >= s).astype(F32) for s in (1, 2, 4)]
    heads = [slice(h * HG_DK, (h + 1) * HG_DK) for h in range(HG_HEADS)]

    for c in range(tb // CHUNK):
        rows = slice(c * CHUNK, (c + 1) * CHUNK)
        for cols in heads:
            ops, dec = _decay_operands(lf_ref[rows, cols], q_ref[rows, cols],
                                       k_ref[rows, cols], scan_masks)
            for i, op in enumerate(ops):
                op_ref[i, rows, cols] = op
            dec_ref[c, :, cols] = dec

        for h, cols in enumerate(heads):
            q_d, k_d, q_a, k_a, q_b, k_b = [op_ref[i, rows, cols] for i in range(1, 7)]
            s_d = _dot_nt(q_d, k_d)
            s_a = _dot_nt(q_a, k_a)
            s_b = _dot_nt(q_b, k_b)
            scores = jnp.where(is_diag, s_d,
                               jnp.where(is_adj, s_a, jnp.where(is_half, s_b, 0.0)))
            p_ref[h] = scores.astype(BF)

        for h, cols in enumerate(heads):
            st = st_ref[h]
            stb_ref[h] = st.astype(BF)
            st_ref[h] = st * dec_ref[c, 0:1, cols] + _dot_tn(v_ref[rows, cols],
                                                            op_ref[7, rows, cols])

        for h, cols in enumerate(heads):
            acc_ref[h] = (_dot_nt(op_ref[0, rows, cols], stb_ref[h])
                          + _dot(p_ref[h], v_ref[rows, cols]))

        for h, cols in enumerate(heads):
            o = acc_ref[h]
            ms = jnp.mean(o * o, axis=-1, keepdims=True)
            o = o * lax.rsqrt(ms + EPS) * gn_ref[:, cols]
            o_ref[rows, cols] = (o * gs_ref[rows, cols].astype(F32)).astype(BF)


def _hgrn2(pk, lf, gn, batch, tb=1024):
    T = lf.shape[0]
    nblk = T // batch // tb
    const = lambda b, i: (0, 0)
    tok = lambda block: pl.BlockSpec((tb, HG_WIDTH), lambda b, i: (b * nblk + i, block))
    code = jnp.asarray(_score_masks())
    return pl.pallas_call(
        functools.partial(_hgrn2_kernel, tb=tb),
        grid=(batch, nblk),
        in_specs=[tok(PK_Q), tok(PK_K), tok(PK_V), tok(0), tok(PK_GS),
                  pl.BlockSpec((1, HG_WIDTH), const),
                  pl.BlockSpec(code.shape, const)],
        out_specs=tok(0),
        out_shape=jax.ShapeDtypeStruct((T, HG_WIDTH), BF),
        scratch_shapes=[pltpu.VMEM((HG_HEADS, HG_DK, HG_DK), F32),
                        pltpu.VMEM((8, tb, HG_WIDTH), BF),
                        pltpu.VMEM((tb // CHUNK, GROUP, HG_WIDTH), F32),
                        pltpu.VMEM((HG_HEADS, CHUNK, CHUNK), BF),
                        pltpu.VMEM((HG_HEADS, HG_DK, HG_DK), BF),
                        pltpu.VMEM((HG_HEADS, CHUNK, HG_DK), F32)],
        compiler_params=pltpu.CompilerParams(
            dimension_semantics=("arbitrary", "arbitrary"), vmem_limit_bytes=VMEM_LIMIT),
        name="hgrn2",
    )(pk, pk, pk, lf, pk, gn, code)


MASK_BIAS = -1e30
PAIR = LANES // ATT_HEAD_DIM


def _swa_bias():
    W = WINDOW
    i = np.arange(W)[:, None]
    j = np.arange(2 * W)[None, :]
    vis = (j > i) & (j <= i + W)
    first = vis & (j >= W)
    return np.where(np.stack([first, vis]), 0.0, MASK_BIAS).astype(np.float32)


def _swa_kernel(sink_ref, bias_ref, q_ref, kp_ref, kc_ref, vp_ref, vc_ref, o_ref,
                s_ref, p_ref, inv_ref):
    W = WINDOW
    lane = lax.broadcasted_iota(jnp.int32, (2 * W, LANES), 1)
    low = lane < ATT_HEAD_DIM
    low_q = lax.broadcasted_iota(jnp.int32, (W, LANES), 1) < ATT_HEAD_DIM

    def placed(t, hk):
        if hk % PAIR == 0:
            lo = jnp.where(low, t, jnp.zeros_like(t))
            return lo, pltpu.roll(lo, ATT_HEAD_DIM, axis=1)
        hi = jnp.where(low, jnp.zeros_like(t), t)
        return pltpu.roll(hi, ATT_HEAD_DIM, axis=1), hi

    def qk_stage(hk):
        tile = slice((hk // PAIR) * LANES, (hk // PAIR + 1) * LANES)
        k_lo, k_hi = placed(jnp.concatenate([kp_ref[:, tile], kc_ref[:, tile]], axis=0), hk)
        for g in range(ATT_GROUP):
            hq = hk * ATT_GROUP + g
            q2 = q_ref[:, (hq // PAIR) * LANES:(hq // PAIR + 1) * LANES]
            s_ref[hq] = _dot_nt(q2, k_lo if hq % PAIR == 0 else k_hi) + bias_ref[0]

    def softmax_stage(hk):
        for g in range(ATT_GROUP):
            hq = hk * ATT_GROUP + g
            s = s_ref[hq]
            sink = jnp.full((W, 1), sink_ref[hq], F32) * LOG2E
            m = jnp.maximum(jnp.max(s, axis=-1, keepdims=True), sink)
            p = jnp.exp2(s - m)
            p_ref[hq] = p.astype(BF)
            inv = 1.0 / (jnp.sum(p, axis=-1, keepdims=True) + jnp.exp2(sink - m))
            inv_ref[hq] = jnp.broadcast_to(inv, (W, LANES))

    def pv_stage(hk):
        tile = slice((hk // PAIR) * LANES, (hk // PAIR + 1) * LANES)
        v_lo, v_hi = placed(jnp.concatenate([vp_ref[:, tile], vc_ref[:, tile]], axis=0), hk)
        for pair in range(ATT_GROUP // PAIR):
            hq = hk * ATT_GROUP + pair * PAIR
            o2 = _dot(p_ref[hq], v_lo) + _dot(p_ref[hq + 1], v_hi)
            o2 = o2 * jnp.where(low_q, inv_ref[hq], inv_ref[hq + 1])
            o_ref[:, (hq // PAIR) * LANES:(hq // PAIR + 1) * LANES] = o2.astype(BF)

    n = ATT_KV_HEADS
    for step in range(n + 2):
        if step < n:
            qk_stage(step)
        if 1 <= step < n + 1:
            softmax_stage(step - 1)
        if step >= 2:
            pv_stage(step - 2)


def _swa(sinks, pk, batch):
    T = pk.shape[0]
    W = WINDOW
    nblk = T // batch // W
    kv_cur = lambda block: pl.BlockSpec((W, ATT_KV_WIDTH), lambda b, n: (b * nblk + n, block))
    kv_prev = lambda block: pl.BlockSpec(
        (W, ATT_KV_WIDTH), lambda b, n: (b * nblk + jnp.maximum(n - 1, 0), block))
    bias = jnp.asarray(_swa_bias())
    return pl.pallas_call(
        _swa_kernel,
        grid=(batch, nblk),
        in_specs=[pl.BlockSpec(memory_space=pltpu.SMEM),
                  pl.BlockSpec((1, W, 2 * W), lambda b, n: (jnp.minimum(n, 1), 0, 0)),
                  pl.BlockSpec((W, ATT_WIDTH), lambda b, n: (b * nblk + n, PK_AQ)),
                  kv_prev(PK_AK), kv_cur(PK_AK), kv_prev(PK_AV), kv_cur(PK_AV)],
        out_specs=pl.BlockSpec((W, ATT_WIDTH), lambda b, n: (b * nblk + n, 0)),
        out_shape=jax.ShapeDtypeStruct((T, ATT_WIDTH), BF),
        scratch_shapes=[pltpu.VMEM((ATT_Q_HEADS, W, 2 * W), F32),
                        pltpu.VMEM((ATT_Q_HEADS, W, 2 * W), BF),
                        pltpu.VMEM((ATT_Q_HEADS, W, LANES), F32)],
        compiler_params=pltpu.CompilerParams(
            dimension_semantics=("arbitrary", "arbitrary"), vmem_limit_bytes=VMEM_LIMIT),
        name="swa",
    )(sinks, bias, pk, pk, pk, pk, pk)


def _merge_kernel(x_ref, oa_ref, ob_ref, ga_ref, gb_ref, wpa_ref, wpb_ref, wo_ref, o_ref):
    ya = _dot(oa_ref[...], wpa_ref[...])
    yb = _dot(ob_ref[...], wpb_ref[...])
    mix = ga_ref[...].astype(F32) * ya + gb_ref[...].astype(F32) * yb
    o_ref[...] = x_ref[...] + _dot(mix.astype(BF), wo_ref[...])


def _merge(x2d, oa, ob, pk, wpa, wpb, wo, tm=1024):
    T = x2d.shape[0]
    const = lambda i: (0, 0)
    tok = lambda block: pl.BlockSpec((tm, D_MODEL), lambda i: (i, block))
    wsp = pl.BlockSpec((D_MODEL, D_MODEL), const)
    return pl.pallas_call(
        _merge_kernel,
        grid=(T // tm,),
        in_specs=[tok(0), tok(0), tok(0), tok(PK_GA), tok(PK_GB), wsp, wsp, wsp],
        out_specs=tok(0),
        out_shape=jax.ShapeDtypeStruct((T, D_MODEL), F32),
        compiler_params=pltpu.CompilerParams(
            dimension_semantics=("arbitrary",), vmem_limit_bytes=VMEM_LIMIT),
        name="merge",
    )(x2d, oa, ob, pk, pk, wpa, wpb, wo)


FFN_COL_CHUNK = FFN_HIDDEN


def _ffn_kernel(x_ref, g_ref, wg_ref, wu_ref, wd_ref, fg_ref, o_ref, act_ref, *, final):
    x = x_ref[...]
    h = x * lax.rsqrt(jnp.mean(x * x, axis=-1, keepdims=True) + EPS) * g_ref[...]
    h = h.astype(BF)
    for c in range(FFN_HIDDEN // FFN_COL_CHUNK):
        cols = slice(c * FFN_COL_CHUNK, (c + 1) * FFN_COL_CHUNK)
        gate = _dot(h, wg_ref[:, cols])
        up = _dot(h, wu_ref[:, cols])
        act_ref[:, cols] = (gate * _sigmoid(gate) * up).astype(BF)
    y = x + _dot(act_ref[...], wd_ref[...])
    if final:
        y = y * lax.rsqrt(jnp.mean(y * y, axis=-1, keepdims=True) + EPS) * fg_ref[...]
    o_ref[...] = y


def _ffn(x2d, g, wg, wu, wd, fg, final, tm=512):
    T = x2d.shape[0]
    row = lambda i: (i, 0)
    const = lambda i: (0, 0)
    tok = pl.BlockSpec((tm, D_MODEL), row)
    vec = pl.BlockSpec((1, D_MODEL), const)
    single = pl.Buffered(1)
    return pl.pallas_call(
        functools.partial(_ffn_kernel, final=final),
        grid=(T // tm,),
        in_specs=[tok, vec,
                  pl.BlockSpec((D_MODEL, FFN_HIDDEN), const, pipeline_mode=single),
                  pl.BlockSpec((D_MODEL, FFN_HIDDEN), const, pipeline_mode=single),
                  pl.BlockSpec((FFN_HIDDEN, D_MODEL), const, pipeline_mode=single),
                  vec],
        out_specs=tok,
        out_shape=jax.ShapeDtypeStruct((T, D_MODEL), F32),
        scratch_shapes=[pltpu.VMEM((tm, FFN_HIDDEN), BF)],
        compiler_params=pltpu.CompilerParams(
            dimension_semantics=("arbitrary",), vmem_limit_bytes=VMEM_LIMIT),
        name="ffn_final" if final else "ffn",
    )(x2d, g, wg, wu, wd, fg)


def _rope_tables(seq):
    inv = ROPE_THETA ** (-jnp.arange(ROPE_HALF, dtype=F32) * 2.0 / ROPE_DIM)
    ang = jnp.arange(seq, dtype=F32)[:, None] * inv[None, :]
    cos, sin = jnp.cos(ang), jnp.sin(ang)
    ones = jnp.ones((seq, ATT_HEAD_DIM - ROPE_DIM), F32)
    zeros = jnp.zeros((seq, ATT_HEAD_DIM - ROPE_DIM), F32)
    zh = jnp.zeros_like(sin)
    cos_h = jnp.concatenate([cos, cos, ones], axis=1)
    s1_h = jnp.concatenate([zh, sin, zeros], axis=1)
    s2_h = jnp.concatenate([-sin, zh, zeros], axis=1)
    rep = LANES // ATT_HEAD_DIM
    return jnp.concatenate([jnp.tile(t, (1, rep)) for t in (cos_h, s1_h, s2_h)], axis=1)


def kernel(x, norm1, w_in, lb_logits, hg_norm, attn_sinks, w_pa, w_pb, w_o,
           norm2, w_gate, w_up, w_down, final_norm):
    B, S, D = x.shape
    T = B * S
    rope_t = _rope_tables(S)
    layer_ids = jnp.arange(DEPTH)[None, :, None]
    sel_all = ((layer_ids >= 1) & (layer_ids <= jnp.arange(DEPTH)[:, None, None])).astype(F32)

    x2d = x.reshape(T, D)
    for l in range(DEPTH):
        pk, lf = _in_proj(x2d, norm1[l][None, :], w_in[l].astype(BF), lb_logits, sel_all[l],
                          rope_t, S)
        o_hg = _hgrn2(pk, lf, hg_norm[l][None, :], B)
        o_at = _swa(attn_sinks[l], pk, B)
        x2d = _merge(x2d, o_hg, o_at, pk,
                     w_pa[l].astype(BF), w_pb[l].astype(BF), w_o[l].astype(BF))
        x2d = _ffn(x2d, norm2[l][None, :], w_gate[l].astype(BF), w_up[l].astype(BF),
                   w_down[l].astype(BF), final_norm[None, :], final=(l == DEPTH - 1))
    return x2d.reshape(B, S, D)
```

```python
import functools

import numpy as np
import jax
import jax.numpy as jnp
from jax import lax
from jax.experimental import pallas as pl
from jax.experimental.pallas import tpu as pltpu

D_MODEL = 1024
DEPTH = 4
HG_HEADS = 8
HG_DK = 128
HG_WIDTH = HG_HEADS * HG_DK
ATT_Q_HEADS = 16
ATT_KV_HEADS = 4
ATT_GROUP = ATT_Q_HEADS // ATT_KV_HEADS
ATT_HEAD_DIM = 64
ATT_WIDTH = ATT_Q_HEADS * ATT_HEAD_DIM
ATT_KV_WIDTH = ATT_KV_HEADS * ATT_HEAD_DIM
WINDOW = 128
ROPE_THETA = 500000.0
ROPE_DIM = ATT_HEAD_DIM // 4
ROPE_HALF = ROPE_DIM // 2
FFN_HIDDEN = ((8 * D_MODEL // 3 + 255) // 256) * 256
EPS = 1e-6
MIN_F = 1e-30
IN_COLS = 4 * HG_WIDTH + ATT_WIDTH + 2 * ATT_KV_WIDTH + 2 * D_MODEL

LANES = 128
GROUP = 8
VMEM_LIMIT = 56 * 1024 * 1024

CHUNK = 64
SUB = 16
LOG2E = 1.4426950408889634
EXP_CLAMP = 80.0

BF = jnp.bfloat16
F32 = jnp.float32

_NT = (((1,), (1,)), ((), ()))
_TN = (((0,), (0,)), ((), ()))


def _dot(a, b):
    return jnp.dot(a, b, preferred_element_type=F32)


def _dot_nt(a, b):
    return lax.dot_general(a, b, _NT, preferred_element_type=F32)


def _dot_tn(a, b):
    return lax.dot_general(a, b, _TN, preferred_element_type=F32)


def _sigmoid(x):
    return 1.0 / (1.0 + jnp.exp(-x))


PK_Q, PK_K, PK_V, PK_GS, PK_AQ, PK_GA, PK_GB = range(7)
PK_AK = 7 * HG_WIDTH // ATT_KV_WIDTH
PK_AV = PK_AK + 1
PK_COLS = 7 * HG_WIDTH + 2 * ATT_KV_WIDTH


def _in_proj_kernel(x_ref, g_ref, w_ref, lbl_ref, sel_ref, rope_ref, pk_ref, lf_ref):
    def out(block, width=HG_WIDTH):
        return pk_ref.at[:, block * width:(block + 1) * width]

    q_ref, k_ref, v_ref, gs_ref = out(PK_Q), out(PK_K), out(PK_V), out(PK_GS)
    aq_ref, ga_ref, gb_ref = out(PK_AQ), out(PK_GA), out(PK_GB)
    ak_ref, av_ref = out(PK_AK, ATT_KV_WIDTH), out(PK_AV, ATT_KV_WIDTH)

    x = x_ref[...]
    h = x * lax.rsqrt(jnp.mean(x * x, axis=-1, keepdims=True) + EPS) * g_ref[...]
    h = h.astype(BF)

    def proj(col, width):
        return _dot(h, w_ref[:, col:col + width])

    lbl = lbl_ref[...]
    e = jnp.exp(lbl - jnp.max(lbl, axis=0, keepdims=True))
    p = e / jnp.sum(e, axis=0, keepdims=True)
    lb = jnp.sum(p * sel_ref[...], axis=0, keepdims=True)

    cos = rope_ref[:, 0:LANES]
    s1 = rope_ref[:, LANES:2 * LANES]
    s2 = rope_ref[:, 2 * LANES:3 * LANES]

    def rope(t, scale):
        out = []
        for j in range(t.shape[1] // LANES):
            tj = t[:, j * LANES:(j + 1) * LANES]
            r = (tj * cos + pltpu.roll(tj, ROPE_HALF, axis=1) * s1
                 + pltpu.roll(tj, LANES - ROPE_HALF, axis=1) * s2)
            out.append(r * scale)
        return jnp.concatenate(out, axis=1)

    c_hq, c_hf, c_hi, c_hg = (i * HG_WIDTH for i in range(4))
    c_aq = 4 * HG_WIDTH
    c_ak = c_aq + ATT_WIDTH
    c_av = c_ak + ATT_KV_WIDTH
    c_ga = c_av + ATT_KV_WIDTH
    c_gb = c_ga + D_MODEL

    z = proj(c_hf, HG_WIDTH)
    f = lb + (1.0 - lb) * _sigmoid(z)
    lf_ref[...] = jnp.log(jnp.maximum(f, MIN_F))
    k_ref[...] = (1.0 - f).astype(BF)

    aq = proj(c_aq, ATT_WIDTH)
    aq_ref[...] = rope(aq, ATT_HEAD_DIM ** -0.5 * LOG2E).astype(BF)

    hq = proj(c_hq, HG_WIDTH)
    q_ref[...] = (hq * _sigmoid(hq)).astype(BF)
    hg = proj(c_hg, HG_WIDTH)
    gs_ref[...] = (hg * _sigmoid(hg)).astype(BF)
    ga_ref[...] = _sigmoid(proj(c_ga, D_MODEL)).astype(BF)
    gb_ref[...] = _sigmoid(proj(c_gb, D_MODEL)).astype(BF)
    ak_ref[...] = rope(proj(c_ak, ATT_KV_WIDTH), 1.0).astype(BF)
    v_ref[...] = proj(c_hi, HG_WIDTH).astype(BF)
    av_ref[...] = proj(c_av, ATT_KV_WIDTH).astype(BF)


def _in_proj(x2d, g, w, lb_logits, sel, rope_t, seq, tm=256):
    T = x2d.shape[0]
    n_pos_blocks = seq // tm
    row = lambda i: (i, 0)
    pos = lambda i: (i % n_pos_blocks, 0)
    const = lambda i: (0, 0)
    return pl.pallas_call(
        _in_proj_kernel,
        grid=(T // tm,),
        in_specs=[
            pl.BlockSpec((tm, D_MODEL), row),
            pl.BlockSpec((1, D_MODEL), const),
            pl.BlockSpec((D_MODEL, IN_COLS), const, pipeline_mode=pl.Buffered(1)),
            pl.BlockSpec((DEPTH, HG_WIDTH), const),
            pl.BlockSpec((DEPTH, 1), const),
            pl.BlockSpec((tm, 3 * LANES), pos),
        ],
        out_specs=(pl.BlockSpec((tm, PK_COLS), row), pl.BlockSpec((tm, HG_WIDTH), row)),
        out_shape=(jax.ShapeDtypeStruct((T, PK_COLS), BF),
                   jax.ShapeDtypeStruct((T, HG_WIDTH), F32)),
        compiler_params=pltpu.CompilerParams(
            dimension_semantics=("arbitrary",), vmem_limit_bytes=VMEM_LIMIT),
        name="in_proj",
    )(x2d, g, w, lb_logits, sel, rope_t)


def _score_masks():
    t = np.arange(CHUNK)[:, None]
    s = np.arange(CHUNK)[None, :]
    st, ss = t // SUB, s // SUB
    half = CHUNK // 2
    m_diag = (st == ss) & (s <= t)
    m_adj = ((st % 2) == 1) & (ss == st - 1)
    m_half = (t >= half) & (s < half)
    code = np.where(m_diag, 1, np.where(m_adj, 2, np.where(m_half, 3, 0)))
    return code.astype(np.int32)


def _decay_operands(lf, q, k, scan_masks):
    assert SUB == 2 * GROUP
    n_g = CHUNK // GROUP
    half_g = n_g // 2
    w = lf * LOG2E
    for s, m in zip((1, 2, 4), scan_masks):
        w = w + pltpu.roll(w, s, axis=0) * m
    wg = [w[g * GROUP:(g + 1) * GROUP] for g in range(n_g)]
    tot = [jnp.broadcast_to(x[GROUP - 1:GROUP], x.shape) for x in wg]
    rem = [t - x for t, x in zip(tot, wg)]

    def plus(x, y):
        return x if y is None else x + y

    def prefix(lo, hi):
        pre, acc = {}, None
        for g in range(lo, hi):
            pre[g] = acc
            acc = plus(tot[g], acc)
        return pre, acc

    def suffix(lo, hi):
        suf, acc = {}, None
        for g in reversed(range(lo, hi)):
            suf[g] = acc
            acc = plus(tot[g], acc)
        return suf

    pre_all, total = prefix(0, n_g)
    suf_all = suffix(0, n_g)
    pre_half, _ = prefix(half_g, n_g)
    suf_half = suffix(0, half_g)
    clamp = EXP_CLAMP * LOG2E

    e_in, e_s, e_qd, e_kd, e_qa, e_ka, e_qb, e_kb = ([] for _ in range(8))
    for g in range(n_g):
        first = g % 2 == 0
        odd_sub = (g // 2) % 2 == 1
        e_in.append(plus(wg[g], pre_all[g]))
        e_s.append(plus(rem[g], suf_all[g]))
        if first:
            e_qd.append(jnp.minimum(-rem[g], clamp))
            e_kd.append(rem[g])
        else:
            e_qd.append(wg[g])
            e_kd.append(jnp.minimum(-wg[g], clamp))
        e_qa.append((wg[g] if first else wg[g] + tot[g - 1]) if odd_sub else None)
        e_ka.append(None if odd_sub else (rem[g] + tot[g + 1] if first else rem[g]))
        e_qb.append(plus(wg[g], pre_half[g]) if g >= half_g else None)
        e_kb.append(plus(rem[g], suf_half[g]) if g < half_g else None)

    def scaled(x, exps):
        parts = []
        for j in range(0, n_g, 2):
            xs = x[j * GROUP:(j + 2) * GROUP]
            if exps[j] is None:
                parts.append(xs)
            else:
                factor = jnp.concatenate([jnp.exp2(exps[j]), jnp.exp2(exps[j + 1])], axis=0)
                parts.append(xs * factor.astype(BF))
        return jnp.concatenate(parts, axis=0)

    ops = (scaled(q, e_in), scaled(q, e_qd), scaled(k, e_kd), scaled(q, e_qa),
           scaled(k, e_ka), scaled(q, e_qb), scaled(k, e_kb), scaled(k, e_s))
    return ops, jnp.exp2(total)


def _hgrn2_kernel(q_ref, k_ref, v_ref, lf_ref, gs_ref, gn_ref, code_ref,
                  o_ref, st_ref, op_ref, dec_ref, p_ref, *, tb):
    @pl.when(pl.program_id(1) == 0)
    def _():
        st_ref[...] = jnp.zeros_like(st_ref)

    code = code_ref[...]
    is_diag = code == 1
    is_adj = code == 2
    is_half = code == 3
    sub = lax.broadcasted_iota(jnp.int32, (CHUNK, HG_DK), 0) % GROUP
    scan_masks = [(sub >= s).astype(F32) for s in (1, 2, 4)]
    heads = [slice(h * HG_DK, (h + 1) * HG_DK) for h in range(HG_HEADS)]

    for c in range(tb // CHUNK):
        rows = slice(c * CHUNK, (c + 1) * CHUNK)
        for cols in heads:
            ops, dec = _decay_operands(lf_ref[rows, cols], q_ref[rows, cols],
                                       k_ref[rows, cols], scan_masks)
            for i, op in enumerate(ops):
                op_ref[i, rows, cols] = op
            dec_ref[c, :, cols] = dec

        for h, cols in enumerate(heads):
            q_d, k_d, q_a, k_a, q_b, k_b = [op_ref[i, rows, cols] for i in range(1, 7)]
            s_d = _dot_nt(q_d, k_d)
            s_a = _dot_nt(q_a, k_a)
            s_b = _dot_nt(q_b, k_b)
            scores = jnp.where(is_diag, s_d,
                               jnp.where(is_adj, s_a, jnp.where(is_half, s_b, 0.0)))
            p_ref[h] = scores.astype(BF)

        for h, cols in enumerate(heads):
            o = (_dot_nt(op_ref[0, rows, cols], st_ref[h].astype(BF))
                 + _dot(p_ref[h], v_ref[rows, cols]))
            ms = jnp.mean(o * o, axis=-1, keepdims=True)
            o = o * lax.rsqrt(ms + EPS) * gn_ref[:, cols]
            o_ref[rows, cols] = (o * gs_ref[rows, cols].astype(F32)).astype(BF)

        for h, cols in enumerate(heads):
            st_ref[h] = st_ref[h] * dec_ref[c, 0:1, cols] + _dot_tn(v_ref[rows, cols],
                                                                    op_ref[7, rows, cols])


def _hgrn2(pk, lf, gn, batch, tb=1024):
    T = lf.shape[0]
    nblk = T // batch // tb
    const = lambda b, i: (0, 0)
    tok = lambda block: pl.BlockSpec((tb, HG_WIDTH), lambda b, i: (b * nblk + i, block))
    code = jnp.asarray(_score_masks())
    return pl.pallas_call(
        functools.partial(_hgrn2_kernel, tb=tb),
        grid=(batch, nblk),
        in_specs=[tok(PK_Q), tok(PK_K), tok(PK_V), tok(0), tok(PK_GS),
                  pl.BlockSpec((1, HG_WIDTH), const),
                  pl.BlockSpec(code.shape, const)],
        out_specs=tok(0),
        out_shape=jax.ShapeDtypeStruct((T, HG_WIDTH), BF),
        scratch_shapes=[pltpu.VMEM((HG_HEADS, HG_DK, HG_DK), F32),
                        pltpu.VMEM((8, tb, HG_WIDTH), BF),
                        pltpu.VMEM((tb // CHUNK, GROUP, HG_WIDTH), F32),
                        pltpu.VMEM((HG_HEADS, CHUNK, CHUNK), BF)],
        compiler_params=pltpu.CompilerParams(
            dimension_semantics=("arbitrary", "arbitrary"), vmem_limit_bytes=VMEM_LIMIT),
        name="hgrn2",
    )(pk, pk, pk, lf, pk, gn, code)


MASK_BIAS = -1e30
PAIR = LANES // ATT_HEAD_DIM


def _swa_bias():
    W = WINDOW
    i = np.arange(W)[:, None]
    j = np.arange(2 * W)[None, :]
    vis = (j > i) & (j <= i + W)
    first = vis & (j >= W)
    return np.where(np.stack([first, vis]), 0.0, MASK_BIAS).astype(np.float32)


def _swa_kernel(sink_ref, bias_ref, q_ref, kp_ref, kc_ref, vp_ref, vc_ref, o_ref,
                s_ref, p_ref, inv_ref):
    W = WINDOW
    lane = lax.broadcasted_iota(jnp.int32, (2 * W, LANES), 1)
    low = lane < ATT_HEAD_DIM
    low_q = lax.broadcasted_iota(jnp.int32, (W, LANES), 1) < ATT_HEAD_DIM

    def placed(t, hk):
        if hk % PAIR == 0:
            lo = jnp.where(low, t, jnp.zeros_like(t))
            return lo, pltpu.roll(lo, ATT_HEAD_DIM, axis=1)
        hi = jnp.where(low, jnp.zeros_like(t), t)
        return pltpu.roll(hi, ATT_HEAD_DIM, axis=1), hi

    def qk_stage(hk):
        tile = slice((hk // PAIR) * LANES, (hk // PAIR + 1) * LANES)
        k_lo, k_hi = placed(jnp.concatenate([kp_ref[:, tile], kc_ref[:, tile]], axis=0), hk)
        for g in range(ATT_GROUP):
            hq = hk * ATT_GROUP + g
            q2 = q_ref[:, (hq // PAIR) * LANES:(hq // PAIR + 1) * LANES]
            s_ref[hq] = _dot_nt(q2, k_lo if hq % PAIR == 0 else k_hi) + bias_ref[0]

    def softmax_stage(hk):
        for g in range(ATT_GROUP):
            hq = hk * ATT_GROUP + g
            s = s_ref[hq]
            sink = jnp.full((W, 1), sink_ref[hq], F32) * LOG2E
            m = jnp.maximum(jnp.max(s, axis=-1, keepdims=True), sink)
            p = jnp.exp2(s - m)
            p_ref[hq] = p.astype(BF)
            inv = 1.0 / (jnp.sum(p, axis=-1, keepdims=True) + jnp.exp2(sink - m))
            inv_ref[hq] = jnp.broadcast_to(inv, (W, LANES))

    def pv_stage(hk):
        tile = slice((hk // PAIR) * LANES, (hk // PAIR + 1) * LANES)
        v_lo, v_hi = placed(jnp.concatenate([vp_ref[:, tile], vc_ref[:, tile]], axis=0), hk)
        for pair in range(ATT_GROUP // PAIR):
            hq = hk * ATT_GROUP + pair * PAIR
            o2 = _dot(p_ref[hq], v_lo) + _dot(p_ref[hq + 1], v_hi)
            o2 = o2 * jnp.where(low_q, inv_ref[hq], inv_ref[hq + 1])
            o_ref[:, (hq // PAIR) * LANES:(hq // PAIR + 1) * LANES] = o2.astype(BF)

    n = ATT_KV_HEADS
    for step in range(n + 2):
        if step < n:
            qk_stage(step)
        if 1 <= step < n + 1:
            softmax_stage(step - 1)
        if step >= 2:
            pv_stage(step - 2)


def _swa(sinks, pk, batch):
    T = pk.shape[0]
    W = WINDOW
    nblk = T // batch // W
    kv_cur = lambda block: pl.BlockSpec((W, ATT_KV_WIDTH), lambda b, n: (b * nblk + n, block))
    kv_prev = lambda block: pl.BlockSpec(
        (W, ATT_KV_WIDTH), lambda b, n: (b * nblk + jnp.maximum(n - 1, 0), block))
    bias = jnp.asarray(_swa_bias())
    return pl.pallas_call(
        _swa_kernel,
        grid=(batch, nblk),
        in_specs=[pl.BlockSpec(memory_space=pltpu.SMEM),
                  pl.BlockSpec((1, W, 2 * W), lambda b, n: (jnp.minimum(n, 1), 0, 0)),
                  pl.BlockSpec((W, ATT_WIDTH), lambda b, n: (b * nblk + n, PK_AQ)),
                  kv_prev(PK_AK), kv_cur(PK_AK), kv_prev(PK_AV), kv_cur(PK_AV)],
        out_specs=pl.BlockSpec((W, ATT_WIDTH), lambda b, n: (b * nblk + n, 0)),
        out_shape=jax.ShapeDtypeStruct((T, ATT_WIDTH), BF),
        scratch_shapes=[pltpu.VMEM((ATT_Q_HEADS, W, 2 * W), F32),
                        pltpu.VMEM((ATT_Q_HEADS, W, 2 * W), BF),
                        pltpu.VMEM((ATT_Q_HEADS, W, LANES), F32)],
        compiler_params=pltpu.CompilerParams(
            dimension_semantics=("arbitrary", "arbitrary"), vmem_limit_bytes=VMEM_LIMIT),
        name="swa",
    )(sinks, bias, pk, pk, pk, pk, pk)


def _merge_kernel(x_ref, oa_ref, ob_ref, ga_ref, gb_ref, wpa_ref, wpb_ref, wo_ref, o_ref):
    ya = _dot(oa_ref[...], wpa_ref[...])
    yb = _dot(ob_ref[...], wpb_ref[...])
    mix = ga_ref[...].astype(F32) * ya + gb_ref[...].astype(F32) * yb
    o_ref[...] = x_ref[...] + _dot(mix.astype(BF), wo_ref[...])


def _merge(x2d, oa, ob, pk, wpa, wpb, wo, tm=1024):
    T = x2d.shape[0]
    const = lambda i: (0, 0)
    tok = lambda block: pl.BlockSpec((tm, D_MODEL), lambda i: (i, block))
    wsp = pl.BlockSpec((D_MODEL, D_MODEL), const)
    return pl.pallas_call(
        _merge_kernel,
        grid=(T // tm,),
        in_specs=[tok(0), tok(0), tok(0), tok(PK_GA), tok(PK_GB), wsp, wsp, wsp],
        out_specs=tok(0),
        out_shape=jax.ShapeDtypeStruct((T, D_MODEL), F32),
        compiler_params=pltpu.CompilerParams(
            dimension_semantics=("arbitrary",), vmem_limit_bytes=VMEM_LIMIT),
        name="merge",
    )(x2d, oa, ob, pk, pk, wpa, wpb, wo)


FFN_COL_CHUNK = FFN_HIDDEN


def _ffn_kernel(x_ref, g_ref, wg_ref, wu_ref, wd_ref, fg_ref, o_ref, act_ref, *, final):
    x = x_ref[...]
    h = x * lax.rsqrt(jnp.mean(x * x, axis=-1, keepdims=True) + EPS) * g_ref[...]
    h = h.astype(BF)
    for c in range(FFN_HIDDEN // FFN_COL_CHUNK):
        cols = slice(c * FFN_COL_CHUNK, (c + 1) * FFN_COL_CHUNK)
        gate = _dot(h, wg_ref[:, cols])
        up = _dot(h, wu_ref[:, cols])
        act_ref[:, cols] = (gate * _sigmoid(gate) * up).astype(BF)
    y = x + _dot(act_ref[...], wd_ref[...])
    if final:
        y = y * lax.rsqrt(jnp.mean(y * y, axis=-1, keepdims=True) + EPS) * fg_ref[...]
    o_ref[...] = y


def _ffn(x2d, g, wg, wu, wd, fg, final, tm=512):
    T = x2d.shape[0]
    row = lambda i: (i, 0)
    const = lambda i: (0, 0)
    tok = pl.BlockSpec((tm, D_MODEL), row)
    vec = pl.BlockSpec((1, D_MODEL), const)
    single = pl.Buffered(1)
    return pl.pallas_call(
        functools.partial(_ffn_kernel, final=final),
        grid=(T // tm,),
        in_specs=[tok, vec,
                  pl.BlockSpec((D_MODEL, FFN_HIDDEN), const, pipeline_mode=single),
                  pl.BlockSpec((D_MODEL, FFN_HIDDEN), const, pipeline_mode=single),
                  pl.BlockSpec((FFN_HIDDEN, D_MODEL), const, pipeline_mode=single),
                  vec],
        out_specs=tok,
        out_shape=jax.ShapeDtypeStruct((T, D_MODEL), F32),
        scratch_shapes=[pltpu.VMEM((tm, FFN_HIDDEN), BF)],
        compiler_params=pltpu.CompilerParams(
            dimension_semantics=("arbitrary",), vmem_limit_bytes=VMEM_LIMIT),
        name="ffn_final" if final else "ffn",
    )(x2d, g, wg, wu, wd, fg)


def _rope_tables(seq):
    inv = ROPE_THETA ** (-jnp.arange(ROPE_HALF, dtype=F32) * 2.0 / ROPE_DIM)
    ang = jnp.arange(seq, dtype=F32)[:, None] * inv[None, :]
    cos, sin = jnp.cos(ang), jnp.sin(ang)
    ones = jnp.ones((seq, ATT_HEAD_DIM - ROPE_DIM), F32)
    zeros = jnp.zeros((seq, ATT_HEAD_DIM - ROPE_DIM), F32)
    zh = jnp.zeros_like(sin)
    cos_h = jnp.concatenate([cos, cos, ones], axis=1)
    s1_h = jnp.concatenate([zh, sin, zeros], axis=1)
    s2_h = jnp.concatenate([-sin, zh, zeros], axis=1)
    rep = LANES // ATT_HEAD_DIM
    return jnp.concatenate([jnp.tile(t, (1, rep)) for t in (cos_h, s1_h, s2_h)], axis=1)


def kernel(x, norm1, w_in, lb_logits, hg_norm, attn_sinks, w_pa, w_pb, w_o,
           norm2, w_gate, w_up, w_down, final_norm):
    B, S, D = x.shape
    T = B * S
    rope_t = _rope_tables(S)
    layer_ids = jnp.arange(DEPTH)[None, :, None]
    sel_all = ((layer_ids >= 1) & (layer_ids <= jnp.arange(DEPTH)[:, None, None])).astype(F32)

    x2d = x.reshape(T, D)
    for l in range(DEPTH):
        pk, lf = _in_proj(x2d, norm1[l][None, :], w_in[l].astype(BF), lb_logits, sel_all[l],
                          rope_t, S)
        o_hg = _hgrn2(pk, lf, hg_norm[l][None, :], B)
        o_at = _swa(attn_sinks[l], pk, B)
        x2d = _merge(x2d, o_hg, o_at, pk,
                     w_pa[l].astype(BF), w_pb[l].astype(BF), w_o[l].astype(BF))
        x2d = _ffn(x2d, norm2[l][None, :], w_gate[l].astype(BF), w_up[l].astype(BF),
                   w_down[l].astype(BF), final_norm[None, :], final=(l == DEPTH - 1))
    return x2d.reshape(B, S, D)
```

```python
import functools

import numpy as np
import jax
import jax.numpy as jnp
from jax import lax
from jax.experimental import pallas as pl
from jax.experimental.pallas import tpu as pltpu

D_MODEL = 1024
DEPTH = 4
HG_HEADS = 8
HG_DK = 128
HG_WIDTH = HG_HEADS * HG_DK
ATT_Q_HEADS = 16
ATT_KV_HEADS = 4
ATT_GROUP = ATT_Q_HEADS // ATT_KV_HEADS
ATT_HEAD_DIM = 64
ATT_WIDTH = ATT_Q_HEADS * ATT_HEAD_DIM
ATT_KV_WIDTH = ATT_KV_HEADS * ATT_HEAD_DIM
WINDOW = 128
ROPE_THETA = 500000.0
ROPE_DIM = ATT_HEAD_DIM // 4
ROPE_HALF = ROPE_DIM // 2
FFN_HIDDEN = ((8 * D_MODEL // 3 + 255) // 256) * 256
EPS = 1e-6
MIN_F = 1e-30
IN_COLS = 4 * HG_WIDTH + ATT_WIDTH + 2 * ATT_KV_WIDTH + 2 * D_MODEL

LANES = 128
GROUP = 8
VMEM_LIMIT = 56 * 1024 * 1024

CHUNK = 64
SUB = 16
LOG2E = 1.4426950408889634
EXP_CLAMP = 80.0

BF = jnp.bfloat16
F32 = jnp.float32

_NT = (((1,), (1,)), ((), ()))
_TN = (((0,), (0,)), ((), ()))


def _dot(a, b):
    return jnp.dot(a, b, preferred_element_type=F32)


def _dot_nt(a, b):
    return lax.dot_general(a, b, _NT, preferred_element_type=F32)


def _dot_tn(a, b):
    return lax.dot_general(a, b, _TN, preferred_element_type=F32)


def _sigmoid(x):
    return 1.0 / (1.0 + jnp.exp(-x))


PK_Q, PK_K, PK_V, PK_GS, PK_AQ, PK_GA, PK_GB = range(7)
PK_AK = 7 * HG_WIDTH // ATT_KV_WIDTH
PK_AV = PK_AK + 1
PK_COLS = 7 * HG_WIDTH + 2 * ATT_KV_WIDTH


def _in_proj_kernel(x_ref, g_ref, w_ref, lbl_ref, sel_ref, rope_ref, pk_ref, lf_ref):
    def out(block, width=HG_WIDTH):
        return pk_ref.at[:, block * width:(block + 1) * width]

    q_ref, k_ref, v_ref, gs_ref = out(PK_Q), out(PK_K), out(PK_V), out(PK_GS)
    aq_ref, ga_ref, gb_ref = out(PK_AQ), out(PK_GA), out(PK_GB)
    ak_ref, av_ref = out(PK_AK, ATT_KV_WIDTH), out(PK_AV, ATT_KV_WIDTH)

    x = x_ref[...]
    h = x * lax.rsqrt(jnp.mean(x * x, axis=-1, keepdims=True) + EPS) * g_ref[...]
    h = h.astype(BF)

    def proj(col, width):
        return _dot(h, w_ref[:, col:col + width])

    lbl = lbl_ref[...]
    e = jnp.exp(lbl - jnp.max(lbl, axis=0, keepdims=True))
    p = e / jnp.sum(e, axis=0, keepdims=True)
    lb = jnp.sum(p * sel_ref[...], axis=0, keepdims=True)

    col = 0
    hq = proj(col, HG_WIDTH); col += HG_WIDTH
    q_ref[...] = (hq * _sigmoid(hq)).astype(BF)

    z = proj(col, HG_WIDTH); col += HG_WIDTH
    f = lb + (1.0 - lb) * _sigmoid(z)
    lf_ref[...] = jnp.log(jnp.maximum(f, MIN_F))
    k_ref[...] = (1.0 - f).astype(BF)

    v_ref[...] = proj(col, HG_WIDTH).astype(BF); col += HG_WIDTH

    hg = proj(col, HG_WIDTH); col += HG_WIDTH
    gs_ref[...] = (hg * _sigmoid(hg)).astype(BF)

    cos = rope_ref[:, 0:LANES]
    s1 = rope_ref[:, LANES:2 * LANES]
    s2 = rope_ref[:, 2 * LANES:3 * LANES]

    def rope(t, scale):
        out = []
        for j in range(t.shape[1] // LANES):
            tj = t[:, j * LANES:(j + 1) * LANES]
            r = (tj * cos + pltpu.roll(tj, ROPE_HALF, axis=1) * s1
                 + pltpu.roll(tj, LANES - ROPE_HALF, axis=1) * s2)
            out.append(r * scale)
        return jnp.concatenate(out, axis=1)

    aq = proj(col, ATT_WIDTH); col += ATT_WIDTH
    aq_ref[...] = rope(aq, ATT_HEAD_DIM ** -0.5 * LOG2E).astype(BF)
    ak = proj(col, ATT_KV_WIDTH); col += ATT_KV_WIDTH
    ak_ref[...] = rope(ak, 1.0).astype(BF)
    av_ref[...] = proj(col, ATT_KV_WIDTH).astype(BF); col += ATT_KV_WIDTH

    ga_ref[...] = _sigmoid(proj(col, D_MODEL)).astype(BF); col += D_MODEL
    gb_ref[...] = _sigmoid(proj(col, D_MODEL)).astype(BF); col += D_MODEL


def _in_proj(x2d, g, w, lb_logits, sel, rope_t, seq, tm=256):
    T = x2d.shape[0]
    n_pos_blocks = seq // tm
    row = lambda i: (i, 0)
    pos = lambda i: (i % n_pos_blocks, 0)
    const = lambda i: (0, 0)
    return pl.pallas_call(
        _in_proj_kernel,
        grid=(T // tm,),
        in_specs=[
            pl.BlockSpec((tm, D_MODEL), row),
            pl.BlockSpec((1, D_MODEL), const),
            pl.BlockSpec((D_MODEL, IN_COLS), const, pipeline_mode=pl.Buffered(1)),
            pl.BlockSpec((DEPTH, HG_WIDTH), const),
            pl.BlockSpec((DEPTH, 1), const),
            pl.BlockSpec((tm, 3 * LANES), pos),
        ],
        out_specs=(pl.BlockSpec((tm, PK_COLS), row), pl.BlockSpec((tm, HG_WIDTH), row)),
        out_shape=(jax.ShapeDtypeStruct((T, PK_COLS), BF),
                   jax.ShapeDtypeStruct((T, HG_WIDTH), F32)),
        compiler_params=pltpu.CompilerParams(
            dimension_semantics=("arbitrary",), vmem_limit_bytes=VMEM_LIMIT),
        name="in_proj",
    )(x2d, g, w, lb_logits, sel, rope_t)


def _score_masks():
    t = np.arange(CHUNK)[:, None]
    s = np.arange(CHUNK)[None, :]
    st, ss = t // SUB, s // SUB
    half = CHUNK // 2
    m_diag = (st == ss) & (s <= t)
    m_adj = ((st % 2) == 1) & (ss == st - 1)
    m_half = (t >= half) & (s < half)
    code = np.where(m_diag, 1, np.where(m_adj, 2, np.where(m_half, 3, 0)))
    return code.astype(np.int32)


def _decay_operands(lf, q, k, scan_masks):
    assert SUB == 2 * GROUP
    n_g = CHUNK // GROUP
    half_g = n_g // 2
    w = lf * LOG2E
    for s, m in zip((1, 2, 4), scan_masks):
        w = w + pltpu.roll(w, s, axis=0) * m
    wg = [w[g * GROUP:(g + 1) * GROUP] for g in range(n_g)]
    tot = [jnp.broadcast_to(x[GROUP - 1:GROUP], x.shape) for x in wg]
    rem = [t - x for t, x in zip(tot, wg)]

    def plus(x, y):
        return x if y is None else x + y

    def prefix(lo, hi):
        pre, acc = {}, None
        for g in range(lo, hi):
            pre[g] = acc
            acc = plus(tot[g], acc)
        return pre, acc

    def suffix(lo, hi):
        suf, acc = {}, None
        for g in reversed(range(lo, hi)):
            suf[g] = acc
            acc = plus(tot[g], acc)
        return suf

    pre_all, total = prefix(0, n_g)
    suf_all = suffix(0, n_g)
    pre_half, _ = prefix(half_g, n_g)
    suf_half = suffix(0, half_g)
    clamp = EXP_CLAMP * LOG2E

    e_in, e_s, e_qd, e_kd, e_qa, e_ka, e_qb, e_kb = ([] for _ in range(8))
    for g in range(n_g):
        first = g % 2 == 0
        odd_sub = (g // 2) % 2 == 1
        e_in.append(plus(wg[g], pre_all[g]))
        e_s.append(plus(rem[g], suf_all[g]))
        if first:
            e_qd.append(jnp.minimum(-rem[g], clamp))
            e_kd.append(rem[g])
        else:
            e_qd.append(wg[g])
            e_kd.append(jnp.minimum(-wg[g], clamp))
        e_qa.append((wg[g] if first else wg[g] + tot[g - 1]) if odd_sub else None)
        e_ka.append(None if odd_sub else (rem[g] + tot[g + 1] if first else rem[g]))
        e_qb.append(plus(wg[g], pre_half[g]) if g >= half_g else None)
        e_kb.append(plus(rem[g], suf_half[g]) if g < half_g else None)

    def scaled(x, exps):
        parts = []
        for j in range(0, n_g, 2):
            xs = x[j * GROUP:(j + 2) * GROUP]
            if exps[j] is None:
                parts.append(xs)
            else:
                factor = jnp.concatenate([jnp.exp2(exps[j]), jnp.exp2(exps[j + 1])], axis=0)
                parts.append(xs * factor.astype(BF))
        return jnp.concatenate(parts, axis=0)

    ops = (scaled(q, e_in), scaled(q, e_qd), scaled(k, e_kd), scaled(q, e_qa),
           scaled(k, e_ka), scaled(q, e_qb), scaled(k, e_kb), scaled(k, e_s))
    return ops, jnp.exp2(total)


def _hgrn2_kernel(q_ref, k_ref, v_ref, lf_ref, gs_ref, gn_ref, code_ref,
                  o_ref, st_ref, op_ref, dec_ref, p_ref, stb_ref, acc_ref, *, tb):
    @pl.when(pl.program_id(1) == 0)
    def _():
        st_ref[...] = jnp.zeros_like(st_ref)

    code = code_ref[...]
    is_diag = code == 1
    is_adj = code == 2
    is_half = code == 3
    sub = lax.broadcasted_iota(jnp.int32, (CHUNK, HG_DK), 0) % GROUP
    scan_masks = [(sub >= s).astype(F32) for s in (1, 2, 4)]
    heads = [slice(h * HG_DK, (h + 1) * HG_DK) for h in range(HG_HEADS)]

    for c in range(tb // CHUNK):
        rows = slice(c * CHUNK, (c + 1) * CHUNK)
        for cols in heads:
            ops, dec = _decay_operands(lf_ref[rows, cols], q_ref[rows, cols],
                                       k_ref[rows, cols], scan_masks)
            for i, op in enumerate(ops):
                op_ref[i, rows, cols] = op
            dec_ref[c, :, cols] = dec

        for h, cols in enumerate(heads):
            q_d, k_d, q_a, k_a, q_b, k_b = [op_ref[i, rows, cols] for i in range(1, 7)]
            s_d = _dot_nt(q_d, k_d)
            s_a = _dot_nt(q_a, k_a)
            s_b = _dot_nt(q_b, k_b)
            scores = jnp.where(is_diag, s_d,
                               jnp.where(is_adj, s_a, jnp.where(is_half, s_b, 0.0)))
            p_ref[h] = scores.astype(BF)

        for h, cols in enumerate(heads):
            st = st_ref[h]
            stb_ref[h] = st.astype(BF)
            st_ref[h] = st * dec_ref[c, 0:1, cols] + _dot_tn(v_ref[rows, cols],
                                                            op_ref[7, rows, cols])

        for h, cols in enumerate(heads):
            acc_ref[h] = (_dot_nt(op_ref[0, rows, cols], stb_ref[h])
                          + _dot(p_ref[h], v_ref[rows, cols]))

        for h, cols in enumerate(heads):
            o = acc_ref[h]
            ms = jnp.mean(o * o, axis=-1, keepdims=True)
            o = o * lax.rsqrt(ms + EPS) * gn_ref[:, cols]
            o_ref[rows, cols] = (o * gs_ref[rows, cols].astype(F32)).astype(BF)


def _hgrn2(pk, lf, gn, batch, tb=1024):
    T = lf.shape[0]
    nblk = T // batch // tb
    const = lambda b, i: (0, 0)
    tok = lambda block: pl.BlockSpec((tb, HG_WIDTH), lambda b, i: (b * nblk + i, block))
    code = jnp.asarray(_score_masks())
    return pl.pallas_call(
        functools.partial(_hgrn2_kernel, tb=tb),
        grid=(batch, nblk),
        in_specs=[tok(PK_Q), tok(PK_K), tok(PK_V), tok(0), tok(PK_GS),
                  pl.BlockSpec((1, HG_WIDTH), const),
                  pl.BlockSpec(code.shape, const)],
        out_specs=tok(0),
        out_shape=jax.ShapeDtypeStruct((T, HG_WIDTH), BF),
        scratch_shapes=[pltpu.VMEM((HG_HEADS, HG_DK, HG_DK), F32),
                        pltpu.VMEM((8, tb, HG_WIDTH), BF),
                        pltpu.VMEM((tb // CHUNK, GROUP, HG_WIDTH), F32),
                        pltpu.VMEM((HG_HEADS, CHUNK, CHUNK), BF),
                        pltpu.VMEM((HG_HEADS, HG_DK, HG_DK), BF),
                        pltpu.VMEM((HG_HEADS, CHUNK, HG_DK), F32)],
        compiler_params=pltpu.CompilerParams(
            dimension_semantics=("arbitrary", "arbitrary"), vmem_limit_bytes=VMEM_LIMIT),
        name="hgrn2",
    )(pk, pk, pk, lf, pk, gn, code)


MASK_BIAS = -1e30
PAIR = LANES // ATT_HEAD_DIM


def _swa_bias():
    W = WINDOW
    i = np.arange(W)[:, None]
    j = np.arange(2 * W)[None, :]
    vis = (j > i) & (j <= i + W)
    first = vis & (j >= W)
    return np.where(np.stack([first, vis]), 0.0, MASK_BIAS).astype(np.float32)


def _swa_kernel(sink_ref, bias_ref, q_ref, kp_ref, kc_ref, vp_ref, vc_ref, o_ref,
                s_ref, p_ref, inv_ref):
    W = WINDOW
    lane = lax.broadcasted_iota(jnp.int32, (2 * W, LANES), 1)
    low = lane < ATT_HEAD_DIM
    low_q = lax.broadcasted_iota(jnp.int32, (W, LANES), 1) < ATT_HEAD_DIM

    def placed(t, hk):
        if hk % PAIR == 0:
            lo = jnp.where(low, t, jnp.zeros_like(t))
            return lo, pltpu.roll(lo, ATT_HEAD_DIM, axis=1)
        hi = jnp.where(low, jnp.zeros_like(t), t)
        return pltpu.roll(hi, ATT_HEAD_DIM, axis=1), hi

    def qk_stage(hk):
        tile = slice((hk // PAIR) * LANES, (hk // PAIR + 1) * LANES)
        k_lo, k_hi = placed(jnp.concatenate([kp_ref[:, tile], kc_ref[:, tile]], axis=0), hk)
        for g in range(ATT_GROUP):
            hq = hk * ATT_GROUP + g
            q2 = q_ref[:, (hq // PAIR) * LANES:(hq // PAIR + 1) * LANES]
            s_ref[hq] = _dot_nt(q2, k_lo if hq % PAIR == 0 else k_hi) + bias_ref[0]

    def softmax_stage(hk):
        for g in range(ATT_GROUP):
            hq = hk * ATT_GROUP + g
            s = s_ref[hq]
            sink = jnp.full((W, 1), sink_ref[hq], F32) * LOG2E
            m = jnp.maximum(jnp.max(s, axis=-1, keepdims=True), sink)
            p = jnp.exp2(s - m)
            p_ref[hq] = p.astype(BF)
            inv = 1.0 / (jnp.sum(p, axis=-1, keepdims=True) + jnp.exp2(sink - m))
            inv_ref[hq] = jnp.broadcast_to(inv, (W, LANES))

    def pv_stage(hk):
        tile = slice((hk // PAIR) * LANES, (hk // PAIR + 1) * LANES)
        v_lo, v_hi = placed(jnp.concatenate([vp_ref[:, tile], vc_ref[:, tile]], axis=0), hk)
        for pair in range(ATT_GROUP // PAIR):
            hq = hk * ATT_GROUP + pair * PAIR
            o2 = _dot(p_ref[hq], v_lo) + _dot(p_ref[hq + 1], v_hi)
            o2 = o2 * jnp.where(low_q, inv_ref[hq], inv_ref[hq + 1])
            o_ref[:, (hq // PAIR) * LANES:(hq // PAIR + 1) * LANES] = o2.astype(BF)

    n = ATT_KV_HEADS
    for step in range(n + 2):
        if step < n:
            qk_stage(step)
        if 1 <= step < n + 1:
            softmax_stage(step - 1)
        if step >= 2:
            pv_stage(step - 2)


def _swa(sinks, pk, batch):
    T = pk.shape[0]
    W = WINDOW
    nblk = T // batch // W
    kv_cur = lambda block: pl.BlockSpec((W, ATT_KV_WIDTH), lambda b, n: (b * nblk + n, block))
    kv_prev = lambda block: pl.BlockSpec(
        (W, ATT_KV_WIDTH), lambda b, n: (b * nblk + jnp.maximum(n - 1, 0), block))
    bias = jnp.asarray(_swa_bias())
    return pl.pallas_call(
        _swa_kernel,
        grid=(batch, nblk),
        in_specs=[pl.BlockSpec(memory_space=pltpu.SMEM),
                  pl.BlockSpec((1, W, 2 * W), lambda b, n: (jnp.minimum(n, 1), 0, 0)),
                  pl.BlockSpec((W, ATT_WIDTH), lambda b, n: (b * nblk + n, PK_AQ)),
                  kv_prev(PK_AK), kv_cur(PK_AK), kv_prev(PK_AV), kv_cur(PK_AV)],
        out_specs=pl.BlockSpec((W, ATT_WIDTH), lambda b, n: (b * nblk + n, 0)),
        out_shape=jax.ShapeDtypeStruct((T, ATT_WIDTH), BF),
        scratch_shapes=[pltpu.VMEM((ATT_Q_HEADS, W, 2 * W), F32),
                        pltpu.VMEM((ATT_Q_HEADS, W, 2 * W), BF),
                        pltpu.VMEM((ATT_Q_HEADS, W, LANES), F32)],
        compiler_params=pltpu.CompilerParams(
            dimension_semantics=("arbitrary", "arbitrary"), vmem_limit_bytes=VMEM_LIMIT),
        name="swa",
    )(sinks, bias, pk, pk, pk, pk, pk)


def _merge_kernel(x_ref, oa_ref, ob_ref, ga_ref, gb_ref, wpa_ref, wpb_ref, wo_ref, o_ref):
    ya = _dot(oa_ref[...], wpa_ref[...])
    yb = _dot(ob_ref[...], wpb_ref[...])
    mix = ga_ref[...].astype(F32) * ya + gb_ref[...].astype(F32) * yb
    o_ref[...] = x_ref[...] + _dot(mix.astype(BF), wo_ref[...])


def _merge(x2d, oa, ob, pk, wpa, wpb, wo, tm=1024):
    T = x2d.shape[0]
    const = lambda i: (0, 0)
    tok = lambda block: pl.BlockSpec((tm, D_MODEL), lambda i: (i, block))
    wsp = pl.BlockSpec((D_MODEL, D_MODEL), const)
    return pl.pallas_call(
        _merge_kernel,
        grid=(T // tm,),
        in_specs=[tok(0), tok(0), tok(0), tok(PK_GA), tok(PK_GB), wsp, wsp, wsp],
        out_specs=tok(0),
        out_shape=jax.ShapeDtypeStruct((T, D_MODEL), F32),
        compiler_params=pltpu.CompilerParams(
            dimension_semantics=("arbitrary",), vmem_limit_bytes=VMEM_LIMIT),
        name="merge",
    )(x2d, oa, ob, pk, pk, wpa, wpb, wo)


FFN_COL_CHUNK = FFN_HIDDEN


def _ffn_kernel(x_ref, g_ref, wg_ref, wu_ref, wd_ref, fg_ref, o_ref, act_ref, *, final):
    x = x_ref[...]
    h = x * lax.rsqrt(jnp.mean(x * x, axis=-1, keepdims=True) + EPS) * g_ref[...]
    h = h.astype(BF)
    for c in range(FFN_HIDDEN // FFN_COL_CHUNK):
        cols = slice(c * FFN_COL_CHUNK, (c + 1) * FFN_COL_CHUNK)
        gate = _dot(h, wg_ref[:, cols])
        up = _dot(h, wu_ref[:, cols])
        act_ref[:, cols] = (gate * _sigmoid(gate) * up).astype(BF)
    y = x + _dot(act_ref[...], wd_ref[...])
    if final:
        y = y * lax.rsqrt(jnp.mean(y * y, axis=-1, keepdims=True) + EPS) * fg_ref[...]
    o_ref[...] = y


def _ffn(x2d, g, wg, wu, wd, fg, final, tm=512):
    T = x2d.shape[0]
    row = lambda i: (i, 0)
    const = lambda i: (0, 0)
    tok = pl.BlockSpec((tm, D_MODEL), row)
    vec = pl.BlockSpec((1, D_MODEL), const)
    single = pl.Buffered(1)
    return pl.pallas_call(
        functools.partial(_ffn_kernel, final=final),
        grid=(T // tm,),
        in_specs=[tok, vec,
                  pl.BlockSpec((D_MODEL, FFN_HIDDEN), const, pipeline_mode=single),
                  pl.BlockSpec((D_MODEL, FFN_HIDDEN), const, pipeline_mode=single),
                  pl.BlockSpec((FFN_HIDDEN, D_MODEL), const, pipeline_mode=single),
                  vec],
        out_specs=tok,
        out_shape=jax.ShapeDtypeStruct((T, D_MODEL), F32),
        scratch_shapes=[pltpu.VMEM((tm, FFN_HIDDEN), BF)],
        compiler_params=pltpu.CompilerParams(
            dimension_semantics=("arbitrary",), vmem_limit_bytes=VMEM_LIMIT),
        name="ffn_final" if final else "ffn",
    )(x2d, g, wg, wu, wd, fg)


def _rope_tables(seq):
    inv = ROPE_THETA ** (-jnp.arange(ROPE_HALF, dtype=F32) * 2.0 / ROPE_DIM)
    ang = jnp.arange(seq, dtype=F32)[:, None] * inv[None, :]
    cos, sin = jnp.cos(ang), jnp.sin(ang)
    ones = jnp.ones((seq, ATT_HEAD_DIM - ROPE_DIM), F32)
    zeros = jnp.zeros((seq, ATT_HEAD_DIM - ROPE_DIM), F32)
    zh = jnp.zeros_like(sin)
    cos_h = jnp.concatenate([cos, cos, ones], axis=1)
    s1_h = jnp.concatenate([zh, sin, zeros], axis=1)
    s2_h = jnp.concatenate([-sin, zh, zeros], axis=1)
    rep = LANES // ATT_HEAD_DIM
    return jnp.concatenate([jnp.tile(t, (1, rep)) for t in (cos_h, s1_h, s2_h)], axis=1)


def kernel(x, norm1, w_in, lb_logits, hg_norm, attn_sinks, w_pa, w_pb, w_o,
           norm2, w_gate, w_up, w_down, final_norm):
    B, S, D = x.shape
    T = B * S
    rope_t = _rope_tables(S)
    layer_ids = jnp.arange(DEPTH)[None, :, None]
    sel_all = ((layer_ids >= 1) & (layer_ids <= jnp.arange(DEPTH)[:, None, None])).astype(F32)

    x2d = x.reshape(T, D)
    for l in range(DEPTH):
        pk, lf = _in_proj(x2d, norm1[l][None, :], w_in[l].astype(BF), lb_logits, sel_all[l],
                          rope_t, S)
        o_hg = _hgrn2(pk, lf, hg_norm[l][None, :], B)
        o_at = _swa(attn_sinks[l], pk, B)
        x2d = _merge(x2d, o_hg, o_at, pk,
                     w_pa[l].astype(BF), w_pb[l].astype(BF), w_o[l].astype(BF))
        x2d = _ffn(x2d, norm2[l][None, :], w_gate[l].astype(BF), w_up[l].astype(BF),
                   w_down[l].astype(BF), final_norm[None, :], final=(l == DEPTH - 1))
    return x2d.reshape(B, S, D)
```

```python
import functools

import numpy as np
import jax
import jax.numpy as jnp
from jax import lax
from jax.experimental import pallas as pl
from jax.experimental.pallas import tpu as pltpu

D_MODEL = 1024
DEPTH = 4
HG_HEADS = 8
HG_DK = 128
HG_WIDTH = HG_HEADS * HG_DK
ATT_Q_HEADS = 16
ATT_KV_HEADS = 4
ATT_GROUP = ATT_Q_HEADS // ATT_KV_HEADS
ATT_HEAD_DIM = 64
ATT_WIDTH = ATT_Q_HEADS * ATT_HEAD_DIM
ATT_KV_WIDTH = ATT_KV_HEADS * ATT_HEAD_DIM
WINDOW = 128
ROPE_THETA = 500000.0
ROPE_DIM = ATT_HEAD_DIM // 4
ROPE_HALF = ROPE_DIM // 2
FFN_HIDDEN = ((8 * D_MODEL // 3 + 255) // 256) * 256
EPS = 1e-6
MIN_F = 1e-30
IN_COLS = 4 * HG_WIDTH + ATT_WIDTH + 2 * ATT_KV_WIDTH + 2 * D_MODEL

LANES = 128
GROUP = 8
VMEM_LIMIT = 56 * 1024 * 1024

CHUNK = 64
SUB = 16
LOG2E = 1.4426950408889634
EXP_CLAMP = 80.0

BF = jnp.bfloat16
F32 = jnp.float32

_NT = (((1,), (1,)), ((), ()))
_TN = (((0,), (0,)), ((), ()))


def _dot(a, b):
    return jnp.dot(a, b, preferred_element_type=F32)


def _dot_nt(a, b):
    return lax.dot_general(a, b, _NT, preferred_element_type=F32)


def _dot_tn(a, b):
    return lax.dot_general(a, b, _TN, preferred_element_type=F32)


def _sigmoid(x):
    return 1.0 / (1.0 + jnp.exp(-x))


PK_Q, PK_K, PK_V, PK_GS, PK_AQ, PK_GA, PK_GB = range(7)
PK_AK = 7 * HG_WIDTH // ATT_KV_WIDTH
PK_AV = PK_AK + 1
PK_COLS = 7 * HG_WIDTH + 2 * ATT_KV_WIDTH


def _in_proj_kernel(x_ref, g_ref, w_ref, lbl_ref, sel_ref, rope_ref, pk_ref, lf_ref):
    def out(block, width=HG_WIDTH):
        return pk_ref.at[:, block * width:(block + 1) * width]

    q_ref, k_ref, v_ref, gs_ref = out(PK_Q), out(PK_K), out(PK_V), out(PK_GS)
    aq_ref, ga_ref, gb_ref = out(PK_AQ), out(PK_GA), out(PK_GB)
    ak_ref, av_ref = out(PK_AK, ATT_KV_WIDTH), out(PK_AV, ATT_KV_WIDTH)

    x = x_ref[...]
    h = x * lax.rsqrt(jnp.mean(x * x, axis=-1, keepdims=True) + EPS) * g_ref[...]
    h = h.astype(BF)

    def proj(col, width):
        return _dot(h, w_ref[:, col:col + width])

    lbl = lbl_ref[...]
    e = jnp.exp(lbl - jnp.max(lbl, axis=0, keepdims=True))
    p = e / jnp.sum(e, axis=0, keepdims=True)
    lb = jnp.sum(p * sel_ref[...], axis=0, keepdims=True)

    col = 0
    hq = proj(col, HG_WIDTH); col += HG_WIDTH
    q_ref[...] = (hq * _sigmoid(hq)).astype(BF)

    z = proj(col, HG_WIDTH); col += HG_WIDTH
    f = lb + (1.0 - lb) * _sigmoid(z)
    lf_ref[...] = jnp.log(jnp.maximum(f, MIN_F))
    k_ref[...] = (1.0 - f).astype(BF)

    v_ref[...] = proj(col, HG_WIDTH).astype(BF); col += HG_WIDTH

    hg = proj(col, HG_WIDTH); col += HG_WIDTH
    gs_ref[...] = (hg * _sigmoid(hg)).astype(BF)

    cos = rope_ref[:, 0:LANES]
    s1 = rope_ref[:, LANES:2 * LANES]
    s2 = rope_ref[:, 2 * LANES:3 * LANES]

    def rope(t, scale):
        out = []
        for j in range(t.shape[1] // LANES):
            tj = t[:, j * LANES:(j + 1) * LANES]
            r = (tj * cos + pltpu.roll(tj, ROPE_HALF, axis=1) * s1
                 + pltpu.roll(tj, LANES - ROPE_HALF, axis=1) * s2)
            out.append(r * scale)
        return jnp.concatenate(out, axis=1)

    aq = proj(col, ATT_WIDTH); col += ATT_WIDTH
    aq_ref[...] = rope(aq, ATT_HEAD_DIM ** -0.5 * LOG2E).astype(BF)
    ak = proj(col, ATT_KV_WIDTH); col += ATT_KV_WIDTH
    ak_ref[...] = rope(ak, 1.0).astype(BF)
    av_ref[...] = proj(col, ATT_KV_WIDTH).astype(BF); col += ATT_KV_WIDTH

    ga_ref[...] = _sigmoid(proj(col, D_MODEL)).astype(BF); col += D_MODEL
    gb_ref[...] = _sigmoid(proj(col, D_MODEL)).astype(BF); col += D_MODEL


def _in_proj(x2d, g, w, lb_logits, sel, rope_t, seq, tm=256):
    T = x2d.shape[0]
    n_pos_blocks = seq // tm
    row = lambda i: (i, 0)
    pos = lambda i: (i % n_pos_blocks, 0)
    const = lambda i: (0, 0)
    return pl.pallas_call(
        _in_proj_kernel,
        grid=(T // tm,),
        in_specs=[
            pl.BlockSpec((tm, D_MODEL), row),
            pl.BlockSpec((1, D_MODEL), const),
            pl.BlockSpec((D_MODEL, IN_COLS), const, pipeline_mode=pl.Buffered(1)),
            pl.BlockSpec((DEPTH, HG_WIDTH), const),
            pl.BlockSpec((DEPTH, 1), const),
            pl.BlockSpec((tm, 3 * LANES), pos),
        ],
        out_specs=(pl.BlockSpec((tm, PK_COLS), row), pl.BlockSpec((tm, HG_WIDTH), row)),
        out_shape=(jax.ShapeDtypeStruct((T, PK_COLS), BF),
                   jax.ShapeDtypeStruct((T, HG_WIDTH), F32)),
        compiler_params=pltpu.CompilerParams(
            dimension_semantics=("arbitrary",), vmem_limit_bytes=VMEM_LIMIT),
        name="in_proj",
    )(x2d, g, w, lb_logits, sel, rope_t)


def _score_masks():
    t = np.arange(CHUNK)[:, None]
    s = np.arange(CHUNK)[None, :]
    st, ss = t // SUB, s // SUB
    half = CHUNK // 2
    m_diag = (st == ss) & (s <= t)
    m_adj = ((st % 2) == 1) & (ss == st - 1)
    m_half = (t >= half) & (s < half)
    code = np.where(m_diag, 1, np.where(m_adj, 2, np.where(m_half, 3, 0)))
    return code.astype(np.int32)


def _decay_operands(lf, q, k, scan_masks):
    assert SUB == 2 * GROUP
    n_g = CHUNK // GROUP
    half_g = n_g // 2
    w = lf * LOG2E
    for s, m in zip((1, 2, 4), scan_masks):
        w = w + pltpu.roll(w, s, axis=0) * m
    wg = [w[g * GROUP:(g + 1) * GROUP] for g in range(n_g)]
    tot = [jnp.broadcast_to(x[GROUP - 1:GROUP], x.shape) for x in wg]
    rem = [t - x for t, x in zip(tot, wg)]

    def plus(x, y):
        return x if y is None else x + y

    def prefix(lo, hi):
        pre, acc = {}, None
        for g in range(lo, hi):
            pre[g] = acc
            acc = plus(tot[g], acc)
        return pre, acc

    def suffix(lo, hi):
        suf, acc = {}, None
        for g in reversed(range(lo, hi)):
            suf[g] = acc
            acc = plus(tot[g], acc)
        return suf

    pre_all, total = prefix(0, n_g)
    suf_all = suffix(0, n_g)
    pre_half, _ = prefix(half_g, n_g)
    suf_half = suffix(0, half_g)
    clamp = EXP_CLAMP * LOG2E

    e_in, e_s, e_qd, e_kd, e_qa, e_ka, e_qb, e_kb = ([] for _ in range(8))
    for g in range(n_g):
        first = g % 2 == 0
        odd_sub = (g // 2) % 2 == 1
        e_in.append(plus(wg[g], pre_all[g]))
        e_s.append(plus(rem[g], suf_all[g]))
        if first:
            e_qd.append(jnp.minimum(-rem[g], clamp))
            e_kd.append(rem[g])
        else:
            e_qd.append(wg[g])
            e_kd.append(jnp.minimum(-wg[g], clamp))
        e_qa.append((wg[g] if first else wg[g] + tot[g - 1]) if odd_sub else None)
        e_ka.append(None if odd_sub else (rem[g] + tot[g + 1] if first else rem[g]))
        e_qb.append(plus(wg[g], pre_half[g]) if g >= half_g else None)
        e_kb.append(plus(rem[g], suf_half[g]) if g < half_g else None)

    def scaled(x, exps):
        factor = jnp.concatenate(
            [jnp.ones((GROUP, x.shape[1]), F32) if e is None else jnp.exp2(e) for e in exps],
            axis=0)
        return x * factor.astype(BF)

    ops = (scaled(q, e_in), scaled(q, e_qd), scaled(k, e_kd), scaled(q, e_qa),
           scaled(k, e_ka), scaled(q, e_qb), scaled(k, e_kb), scaled(k, e_s))
    return ops, jnp.exp2(total)


def _hgrn2_kernel(q_ref, k_ref, v_ref, lf_ref, gs_ref, gn_ref, code_ref,
                  o_ref, st_ref, op_ref, dec_ref, p_ref, stb_ref, acc_ref, *, tb):
    @pl.when(pl.program_id(1) == 0)
    def _():
        st_ref[...] = jnp.zeros_like(st_ref)

    code = code_ref[...]
    is_diag = code == 1
    is_adj = code == 2
    is_half = code == 3
    sub = lax.broadcasted_iota(jnp.int32, (CHUNK, HG_DK), 0) % GROUP
    scan_masks = [(sub >= s).astype(F32) for s in (1, 2, 4)]
    heads = [slice(h * HG_DK, (h + 1) * HG_DK) for h in range(HG_HEADS)]

    for c in range(tb // CHUNK):
        rows = slice(c * CHUNK, (c + 1) * CHUNK)
        for h, cols in enumerate(heads):
            ops, dec = _decay_operands(lf_ref[rows, cols], q_ref[rows, cols],
                                       k_ref[rows, cols], scan_masks)
            for i, op in enumerate(ops):
                op_ref[i, h, rows, :] = op
            dec_ref[c, :, cols] = dec

        for h, cols in enumerate(heads):
            q_d, k_d, q_a, k_a, q_b, k_b = [op_ref[i, h, rows, :] for i in range(1, 7)]
            s_d = _dot_nt(q_d, k_d)
            s_a = _dot_nt(q_a, k_a)
            s_b = _dot_nt(q_b, k_b)
            scores = jnp.where(is_diag, s_d,
                               jnp.where(is_adj, s_a, jnp.where(is_half, s_b, 0.0)))
            p_ref[h] = scores.astype(BF)

        for h, cols in enumerate(heads):
            st = st_ref[h]
            stb_ref[h] = st.astype(BF)
            st_ref[h] = st * dec_ref[c, 0:1, cols] + _dot_tn(v_ref[rows, cols],
                                                            op_ref[7, h, rows, :])

        for h, cols in enumerate(heads):
            acc_ref[h] = (_dot_nt(op_ref[0, h, rows, :], stb_ref[h])
                          + _dot(p_ref[h], v_ref[rows, cols]))

        for h, cols in enumerate(heads):
            o = acc_ref[h]
            ms = jnp.mean(o * o, axis=-1, keepdims=True)
            o = o * lax.rsqrt(ms + EPS) * gn_ref[:, cols]
            o_ref[rows, cols] = (o * gs_ref[rows, cols].astype(F32)).astype(BF)


def _hgrn2(pk, lf, gn, batch, tb=1024):
    T = lf.shape[0]
    nblk = T // batch // tb
    const = lambda b, i: (0, 0)
    tok = lambda block: pl.BlockSpec((tb, HG_WIDTH), lambda b, i: (b * nblk + i, block))
    code = jnp.asarray(_score_masks())
    return pl.pallas_call(
        functools.partial(_hgrn2_kernel, tb=tb),
        grid=(batch, nblk),
        in_specs=[tok(PK_Q), tok(PK_K), tok(PK_V), tok(0), tok(PK_GS),
                  pl.BlockSpec((1, HG_WIDTH), const),
                  pl.BlockSpec(code.shape, const)],
        out_specs=tok(0),
        out_shape=jax.ShapeDtypeStruct((T, HG_WIDTH), BF),
        scratch_shapes=[pltpu.VMEM((HG_HEADS, HG_DK, HG_DK), F32),
                        pltpu.VMEM((8, HG_HEADS, tb, HG_DK), BF),
                        pltpu.VMEM((tb // CHUNK, GROUP, HG_WIDTH), F32),
                        pltpu.VMEM((HG_HEADS, CHUNK, CHUNK), BF),
                        pltpu.VMEM((HG_HEADS, HG_DK, HG_DK), BF),
                        pltpu.VMEM((HG_HEADS, CHUNK, HG_DK), F32)],
        compiler_params=pltpu.CompilerParams(
            dimension_semantics=("arbitrary", "arbitrary"), vmem_limit_bytes=VMEM_LIMIT),
        name="hgrn2",
    )(pk, pk, pk, lf, pk, gn, code)


MASK_BIAS = -1e30
PAIR = LANES // ATT_HEAD_DIM


def _swa_bias():
    W = WINDOW
    i = np.arange(W)[:, None]
    j = np.arange(2 * W)[None, :]
    vis = (j > i) & (j <= i + W)
    first = vis & (j >= W)
    return np.where(np.stack([first, vis]), 0.0, MASK_BIAS).astype(np.float32)


def _swa_kernel(sink_ref, bias_ref, q_ref, kp_ref, kc_ref, vp_ref, vc_ref, o_ref,
                s_ref, p_ref, inv_ref):
    W = WINDOW
    lane = lax.broadcasted_iota(jnp.int32, (2 * W, LANES), 1)
    low = lane < ATT_HEAD_DIM
    low_q = lax.broadcasted_iota(jnp.int32, (W, LANES), 1) < ATT_HEAD_DIM

    def placed(t, hk):
        if hk % PAIR == 0:
            lo = jnp.where(low, t, jnp.zeros_like(t))
            return lo, pltpu.roll(lo, ATT_HEAD_DIM, axis=1)
        hi = jnp.where(low, jnp.zeros_like(t), t)
        return pltpu.roll(hi, ATT_HEAD_DIM, axis=1), hi

    def qk_stage(hk):
        tile = slice((hk // PAIR) * LANES, (hk // PAIR + 1) * LANES)
        k_lo, k_hi = placed(jnp.concatenate([kp_ref[:, tile], kc_ref[:, tile]], axis=0), hk)
        for g in range(ATT_GROUP):
            hq = hk * ATT_GROUP + g
            q2 = q_ref[:, (hq // PAIR) * LANES:(hq // PAIR + 1) * LANES]
            s_ref[hq] = _dot_nt(q2, k_lo if hq % PAIR == 0 else k_hi) + bias_ref[0]

    def softmax_stage(hk):
        for g in range(ATT_GROUP):
            hq = hk * ATT_GROUP + g
            s = s_ref[hq]
            sink = jnp.full((W, 1), sink_ref[hq], F32) * LOG2E
            m = jnp.maximum(jnp.max(s, axis=-1, keepdims=True), sink)
            p = jnp.exp2(s - m)
            p_ref[hq] = p.astype(BF)
            inv = 1.0 / (jnp.sum(p, axis=-1, keepdims=True) + jnp.exp2(sink - m))
            inv_ref[hq] = jnp.broadcast_to(inv, (W, LANES))

    def pv_stage(hk):
        tile = slice((hk // PAIR) * LANES, (hk // PAIR + 1) * LANES)
        v_lo, v_hi = placed(jnp.concatenate([vp_ref[:, tile], vc_ref[:, tile]], axis=0), hk)
        for pair in range(ATT_GROUP // PAIR):
            hq = hk * ATT_GROUP + pair * PAIR
            o2 = _dot(p_ref[hq], v_lo) + _dot(p_ref[hq + 1], v_hi)
            o2 = o2 * jnp.where(low_q, inv_ref[hq], inv_ref[hq + 1])
            o_ref[:, (hq // PAIR) * LANES:(hq // PAIR + 1) * LANES] = o2.astype(BF)

    n = ATT_KV_HEADS
    for step in range(n + 2):
        if step < n:
            qk_stage(step)
        if 1 <= step < n + 1:
            softmax_stage(step - 1)
        if step >= 2:
            pv_stage(step - 2)


def _swa(sinks, pk, batch):
    T = pk.shape[0]
    W = WINDOW
    nblk = T // batch // W
    kv_cur = lambda block: pl.BlockSpec((W, ATT_KV_WIDTH), lambda b, n: (b * nblk + n, block))
    kv_prev = lambda block: pl.BlockSpec(
        (W, ATT_KV_WIDTH), lambda b, n: (b * nblk + jnp.maximum(n - 1, 0), block))
    bias = jnp.asarray(_swa_bias())
    return pl.pallas_call(
        _swa_kernel,
        grid=(batch, nblk),
        in_specs=[pl.BlockSpec(memory_space=pltpu.SMEM),
                  pl.BlockSpec((1, W, 2 * W), lambda b, n: (jnp.minimum(n, 1), 0, 0)),
                  pl.BlockSpec((W, ATT_WIDTH), lambda b, n: (b * nblk + n, PK_AQ)),
                  kv_prev(PK_AK), kv_cur(PK_AK), kv_prev(PK_AV), kv_cur(PK_AV)],
        out_specs=pl.BlockSpec((W, ATT_WIDTH), lambda b, n: (b * nblk + n, 0)),
        out_shape=jax.ShapeDtypeStruct((T, ATT_WIDTH), BF),
        scratch_shapes=[pltpu.VMEM((ATT_Q_HEADS, W, 2 * W), F32),
                        pltpu.VMEM((ATT_Q_HEADS, W, 2 * W), BF),
                        pltpu.VMEM((ATT_Q_HEADS, W, LANES), F32)],
        compiler_params=pltpu.CompilerParams(
            dimension_semantics=("arbitrary", "arbitrary"), vmem_limit_bytes=VMEM_LIMIT),
        name="swa",
    )(sinks, bias, pk, pk, pk, pk, pk)


def _merge_kernel(x_ref, oa_ref, ob_ref, ga_ref, gb_ref, wpa_ref, wpb_ref, wo_ref, o_ref):
    ya = _dot(oa_ref[...], wpa_ref[...])
    yb = _dot(ob_ref[...], wpb_ref[...])
    mix = ga_ref[...].astype(F32) * ya + gb_ref[...].astype(F32) * yb
    o_ref[...] = x_ref[...] + _dot(mix.astype(BF), wo_ref[...])


def _merge(x2d, oa, ob, pk, wpa, wpb, wo, tm=1024):
    T = x2d.shape[0]
    const = lambda i: (0, 0)
    tok = lambda block: pl.BlockSpec((tm, D_MODEL), lambda i: (i, block))
    wsp = pl.BlockSpec((D_MODEL, D_MODEL), const)
    return pl.pallas_call(
        _merge_kernel,
        grid=(T // tm,),
        in_specs=[tok(0), tok(0), tok(0), tok(PK_GA), tok(PK_GB), wsp, wsp, wsp],
        out_specs=tok(0),
        out_shape=jax.ShapeDtypeStruct((T, D_MODEL), F32),
        compiler_params=pltpu.CompilerParams(
            dimension_semantics=("arbitrary",), vmem_limit_bytes=VMEM_LIMIT),
        name="merge",
    )(x2d, oa, ob, pk, pk, wpa, wpb, wo)


FFN_COL_CHUNK = FFN_HIDDEN


def _ffn_kernel(x_ref, g_ref, wg_ref, wu_ref, wd_ref, fg_ref, o_ref, act_ref, *, final):
    x = x_ref[...]
    h = x * lax.rsqrt(jnp.mean(x * x, axis=-1, keepdims=True) + EPS) * g_ref[...]
    h = h.astype(BF)
    for c in range(FFN_HIDDEN // FFN_COL_CHUNK):
        cols = slice(c * FFN_COL_CHUNK, (c + 1) * FFN_COL_CHUNK)
        gate = _dot(h, wg_ref[:, cols])
        up = _dot(h, wu_ref[:, cols])
        act_ref[:, cols] = (gate * _sigmoid(gate) * up).astype(BF)
    y = x + _dot(act_ref[...], wd_ref[...])
    if final:
        y = y * lax.rsqrt(jnp.mean(y * y, axis=-1, keepdims=True) + EPS) * fg_ref[...]
    o_ref[...] = y


def _ffn(x2d, g, wg, wu, wd, fg, final, tm=512):
    T = x2d.shape[0]
    row = lambda i: (i, 0)
    const = lambda i: (0, 0)
    tok = pl.BlockSpec((tm, D_MODEL), row)
    vec = pl.BlockSpec((1, D_MODEL), const)
    single = pl.Buffered(1)
    return pl.pallas_call(
        functools.partial(_ffn_kernel, final=final),
        grid=(T // tm,),
        in_specs=[tok, vec,
                  pl.BlockSpec((D_MODEL, FFN_HIDDEN), const, pipeline_mode=single),
                  pl.BlockSpec((D_MODEL, FFN_HIDDEN), const, pipeline_mode=single),
                  pl.BlockSpec((FFN_HIDDEN, D_MODEL), const, pipeline_mode=single),
                  vec],
        out_specs=tok,
        out_shape=jax.ShapeDtypeStruct((T, D_MODEL), F32),
        scratch_shapes=[pltpu.VMEM((tm, FFN_HIDDEN), BF)],
        compiler_params=pltpu.CompilerParams(
            dimension_semantics=("arbitrary",), vmem_limit_bytes=VMEM_LIMIT),
        name="ffn_final" if final else "ffn",
    )(x2d, g, wg, wu, wd, fg)


def _rope_tables(seq):
    inv = ROPE_THETA ** (-jnp.arange(ROPE_HALF, dtype=F32) * 2.0 / ROPE_DIM)
    ang = jnp.arange(seq, dtype=F32)[:, None] * inv[None, :]
    cos, sin = jnp.cos(ang), jnp.sin(ang)
    ones = jnp.ones((seq, ATT_HEAD_DIM - ROPE_DIM), F32)
    zeros = jnp.zeros((seq, ATT_HEAD_DIM - ROPE_DIM), F32)
    zh = jnp.zeros_like(sin)
    cos_h = jnp.concatenate([cos, cos, ones], axis=1)
    s1_h = jnp.concatenate([zh, sin, zeros], axis=1)
    s2_h = jnp.concatenate([-sin, zh, zeros], axis=1)
    rep = LANES // ATT_HEAD_DIM
    return jnp.concatenate([jnp.tile(t, (1, rep)) for t in (cos_h, s1_h, s2_h)], axis=1)


def kernel(x, norm1, w_in, lb_logits, hg_norm, attn_sinks, w_pa, w_pb, w_o,
           norm2, w_gate, w_up, w_down, final_norm):
    B, S, D = x.shape
    T = B * S
    rope_t = _rope_tables(S)
    layer_ids = jnp.arange(DEPTH)[None, :, None]
    sel_all = ((layer_ids >= 1) & (layer_ids <= jnp.arange(DEPTH)[:, None, None])).astype(F32)

    x2d = x.reshape(T, D)
    for l in range(DEPTH):
        pk, lf = _in_proj(x2d, norm1[l][None, :], w_in[l].astype(BF), lb_logits, sel_all[l],
                          rope_t, S)
        o_hg = _hgrn2(pk, lf, hg_norm[l][None, :], B)
        o_at = _swa(attn_sinks[l], pk, B)
        x2d = _merge(x2d, o_hg, o_at, pk,
                     w_pa[l].astype(BF), w_pb[l].astype(BF), w_o[l].astype(BF))
        x2d = _ffn(x2d, norm2[l][None, :], w_gate[l].astype(BF), w_up[l].astype(BF),
                   w_down[l].astype(BF), final_norm[None, :], final=(l == DEPTH - 1))
    return x2d.reshape(B, S, D)
```

```python
import functools

import numpy as np
import jax
import jax.numpy as jnp
from jax import lax
from jax.experimental import pallas as pl
from jax.experimental.pallas import tpu as pltpu

D_MODEL = 1024
DEPTH = 4
HG_HEADS = 8
HG_DK = 128
HG_WIDTH = HG_HEADS * HG_DK
ATT_Q_HEADS = 16
ATT_KV_HEADS = 4
ATT_GROUP = ATT_Q_HEADS // ATT_KV_HEADS
ATT_HEAD_DIM = 64
ATT_WIDTH = ATT_Q_HEADS * ATT_HEAD_DIM
ATT_KV_WIDTH = ATT_KV_HEADS * ATT_HEAD_DIM
WINDOW = 128
ROPE_THETA = 500000.0
ROPE_DIM = ATT_HEAD_DIM // 4
ROPE_HALF = ROPE_DIM // 2
FFN_HIDDEN = ((8 * D_MODEL // 3 + 255) // 256) * 256
EPS = 1e-6
MIN_F = 1e-30
IN_COLS = 4 * HG_WIDTH + ATT_WIDTH + 2 * ATT_KV_WIDTH + 2 * D_MODEL

LANES = 128
GROUP = 8
VMEM_LIMIT = 56 * 1024 * 1024

CHUNK = 64
SUB = 16
LOG2E = 1.4426950408889634
EXP_CLAMP = 80.0

BF = jnp.bfloat16
F32 = jnp.float32

_NT = (((1,), (1,)), ((), ()))
_TN = (((0,), (0,)), ((), ()))


def _dot(a, b):
    return jnp.dot(a, b, preferred_element_type=F32)


def _dot_nt(a, b):
    return lax.dot_general(a, b, _NT, preferred_element_type=F32)


def _dot_tn(a, b):
    return lax.dot_general(a, b, _TN, preferred_element_type=F32)


def _sigmoid(x):
    return 1.0 / (1.0 + jnp.exp(-x))


PK_Q, PK_K, PK_V, PK_GS, PK_AQ, PK_GA, PK_GB = range(7)
PK_AK = 7 * HG_WIDTH // ATT_KV_WIDTH
PK_AV = PK_AK + 1
PK_COLS = 7 * HG_WIDTH + 2 * ATT_KV_WIDTH


def _in_proj_kernel(x_ref, g_ref, w_ref, lbl_ref, sel_ref, rope_ref, pk_ref, lf_ref):
    def out(block, width=HG_WIDTH):
        return pk_ref.at[:, block * width:(block + 1) * width]

    q_ref, k_ref, v_ref, gs_ref = out(PK_Q), out(PK_K), out(PK_V), out(PK_GS)
    aq_ref, ga_ref, gb_ref = out(PK_AQ), out(PK_GA), out(PK_GB)
    ak_ref, av_ref = out(PK_AK, ATT_KV_WIDTH), out(PK_AV, ATT_KV_WIDTH)

    x = x_ref[...]
    h = x * lax.rsqrt(jnp.mean(x * x, axis=-1, keepdims=True) + EPS) * g_ref[...]
    h = h.astype(BF)

    def proj(col, width):
        return _dot(h, w_ref[:, col:col + width])

    lbl = lbl_ref[...]
    e = jnp.exp(lbl - jnp.max(lbl, axis=0, keepdims=True))
    p = e / jnp.sum(e, axis=0, keepdims=True)
    lb = jnp.sum(p * sel_ref[...], axis=0, keepdims=True)

    col = 0
    hq = proj(col, HG_WIDTH); col += HG_WIDTH
    q_ref[...] = (hq * _sigmoid(hq)).astype(BF)

    z = proj(col, HG_WIDTH); col += HG_WIDTH
    f = lb + (1.0 - lb) * _sigmoid(z)
    lf_ref[...] = jnp.log(jnp.maximum(f, MIN_F))
    k_ref[...] = (1.0 - f).astype(BF)

    v_ref[...] = proj(col, HG_WIDTH).astype(BF); col += HG_WIDTH

    hg = proj(col, HG_WIDTH); col += HG_WIDTH
    gs_ref[...] = (hg * _sigmoid(hg)).astype(BF)

    cos = rope_ref[:, 0:LANES]
    s1 = rope_ref[:, LANES:2 * LANES]
    s2 = rope_ref[:, 2 * LANES:3 * LANES]

    def rope(t, scale):
        out = []
        for j in range(t.shape[1] // LANES):
            tj = t[:, j * LANES:(j + 1) * LANES]
            r = (tj * cos + pltpu.roll(tj, ROPE_HALF, axis=1) * s1
                 + pltpu.roll(tj, LANES - ROPE_HALF, axis=1) * s2)
            out.append(r * scale)
        return jnp.concatenate(out, axis=1)

    aq = proj(col, ATT_WIDTH); col += ATT_WIDTH
    aq_ref[...] = rope(aq, ATT_HEAD_DIM ** -0.5 * LOG2E).astype(BF)
    ak = proj(col, ATT_KV_WIDTH); col += ATT_KV_WIDTH
    ak_ref[...] = rope(ak, 1.0).astype(BF)
    av_ref[...] = proj(col, ATT_KV_WIDTH).astype(BF); col += ATT_KV_WIDTH

    ga_ref[...] = _sigmoid(proj(col, D_MODEL)).astype(BF); col += D_MODEL
    gb_ref[...] = _sigmoid(proj(col, D_MODEL)).astype(BF); col += D_MODEL


def _in_proj(x2d, g, w, lb_logits, sel, rope_t, seq, tm=256):
    T = x2d.shape[0]
    n_pos_blocks = seq // tm
    row = lambda i: (i, 0)
    pos = lambda i: (i % n_pos_blocks, 0)
    const = lambda i: (0, 0)
    return pl.pallas_call(
        _in_proj_kernel,
        grid=(T // tm,),
        in_specs=[
            pl.BlockSpec((tm, D_MODEL), row),
            pl.BlockSpec((1, D_MODEL), const),
            pl.BlockSpec((D_MODEL, IN_COLS), const, pipeline_mode=pl.Buffered(1)),
            pl.BlockSpec((DEPTH, HG_WIDTH), const),
            pl.BlockSpec((DEPTH, 1), const),
            pl.BlockSpec((tm, 3 * LANES), pos),
        ],
        out_specs=(pl.BlockSpec((tm, PK_COLS), row), pl.BlockSpec((tm, HG_WIDTH), row)),
        out_shape=(jax.ShapeDtypeStruct((T, PK_COLS), BF),
                   jax.ShapeDtypeStruct((T, HG_WIDTH), F32)),
        compiler_params=pltpu.CompilerParams(
            dimension_semantics=("arbitrary",), vmem_limit_bytes=VMEM_LIMIT),
        name="in_proj",
    )(x2d, g, w, lb_logits, sel, rope_t)


def _score_masks():
    t = np.arange(CHUNK)[:, None]
    s = np.arange(CHUNK)[None, :]
    st, ss = t // SUB, s // SUB
    half = CHUNK // 2
    m_diag = (st == ss) & (s <= t)
    m_adj = ((st % 2) == 1) & (ss == st - 1)
    m_half = (t >= half) & (s < half)
    code = np.where(m_diag, 1, np.where(m_adj, 2, np.where(m_half, 3, 0)))
    return code.astype(np.int32)


def _decay_operands(lf, q, k, scan_masks):
    assert SUB == 2 * GROUP
    n_g = CHUNK // GROUP
    half_g = n_g // 2
    w = lf * LOG2E
    for s, m in zip((1, 2, 4), scan_masks):
        w = w + pltpu.roll(w, s, axis=0) * m
    wg = [w[g * GROUP:(g + 1) * GROUP] for g in range(n_g)]
    tot = [jnp.broadcast_to(x[GROUP - 1:GROUP], x.shape) for x in wg]
    rem = [t - x for t, x in zip(tot, wg)]

    def plus(x, y):
        return x if y is None else x + y

    def prefix(lo, hi):
        pre, acc = {}, None
        for g in range(lo, hi):
            pre[g] = acc
            acc = plus(tot[g], acc)
        return pre, acc

    def suffix(lo, hi):
        suf, acc = {}, None
        for g in reversed(range(lo, hi)):
            suf[g] = acc
            acc = plus(tot[g], acc)
        return suf

    pre_all, total = prefix(0, n_g)
    suf_all = suffix(0, n_g)
    pre_half, _ = prefix(half_g, n_g)
    suf_half = suffix(0, half_g)
    clamp = EXP_CLAMP * LOG2E

    e_in, e_s, e_qd, e_kd, e_qa, e_ka, e_qb, e_kb = ([] for _ in range(8))
    for g in range(n_g):
        first = g % 2 == 0
        odd_sub = (g // 2) % 2 == 1
        e_in.append(plus(wg[g], pre_all[g]))
        e_s.append(plus(rem[g], suf_all[g]))
        if first:
            e_qd.append(jnp.minimum(-rem[g], clamp))
            e_kd.append(rem[g])
        else:
            e_qd.append(wg[g])
            e_kd.append(jnp.minimum(-wg[g], clamp))
        e_qa.append((wg[g] if first else wg[g] + tot[g - 1]) if odd_sub else None)
        e_ka.append(None if odd_sub else (rem[g] + tot[g + 1] if first else rem[g]))
        e_qb.append(plus(wg[g], pre_half[g]) if g >= half_g else None)
        e_kb.append(plus(rem[g], suf_half[g]) if g < half_g else None)

    def scaled(x, exps):
        factor = jnp.concatenate(
            [jnp.ones((GROUP, x.shape[1]), F32) if e is None else jnp.exp2(e) for e in exps],
            axis=0)
        return x * factor.astype(BF)

    ops = (scaled(q, e_in), scaled(q, e_qd), scaled(k, e_kd), scaled(q, e_qa),
           scaled(k, e_ka), scaled(q, e_qb), scaled(k, e_kb), scaled(k, e_s))
    return ops, jnp.exp2(total)


def _hgrn2_kernel(*refs, tb):
    n = HG_HEADS
    q_heads, k_heads, lf_heads = refs[:n], refs[n:2 * n], refs[2 * n:3 * n]
    (v_ref, gs_ref, gn_ref, code_ref,
     o_ref, st_ref, op_ref, dec_ref, p_ref, stb_ref, acc_ref) = refs[3 * n:]

    @pl.when(pl.program_id(1) == 0)
    def _():
        st_ref[...] = jnp.zeros_like(st_ref)

    code = code_ref[...]
    is_diag = code == 1
    is_adj = code == 2
    is_half = code == 3
    sub = lax.broadcasted_iota(jnp.int32, (CHUNK, HG_DK), 0) % GROUP
    scan_masks = [(sub >= s).astype(F32) for s in (1, 2, 4)]
    heads = [slice(h * HG_DK, (h + 1) * HG_DK) for h in range(HG_HEADS)]

    for c in range(tb // CHUNK):
        rows = slice(c * CHUNK, (c + 1) * CHUNK)
        for h, cols in enumerate(heads):
            ops, dec = _decay_operands(lf_heads[h][rows, :], q_heads[h][rows, :],
                                       k_heads[h][rows, :], scan_masks)
            for i, op in enumerate(ops):
                op_ref[i, h, rows, :] = op
            dec_ref[c, :, cols] = dec

        for h, cols in enumerate(heads):
            q_d, k_d, q_a, k_a, q_b, k_b = [op_ref[i, h, rows, :] for i in range(1, 7)]
            s_d = _dot_nt(q_d, k_d)
            s_a = _dot_nt(q_a, k_a)
            s_b = _dot_nt(q_b, k_b)
            scores = jnp.where(is_diag, s_d,
                               jnp.where(is_adj, s_a, jnp.where(is_half, s_b, 0.0)))
            p_ref[h] = scores.astype(BF)

        for h, cols in enumerate(heads):
            st = st_ref[h]
            stb_ref[h] = st.astype(BF)
            st_ref[h] = st * dec_ref[c, 0:1, cols] + _dot_tn(v_ref[rows, cols],
                                                            op_ref[7, h, rows, :])

        for h, cols in enumerate(heads):
            acc_ref[h] = (_dot_nt(op_ref[0, h, rows, :], stb_ref[h])
                          + _dot(p_ref[h], v_ref[rows, cols]))

        for h, cols in enumerate(heads):
            o = acc_ref[h]
            ms = jnp.mean(o * o, axis=-1, keepdims=True)
            o = o * lax.rsqrt(ms + EPS) * gn_ref[:, cols]
            o_ref[rows, cols] = (o * gs_ref[rows, cols].astype(F32)).astype(BF)


def _hgrn2(pk, lf, gn, batch, tb=1024):
    T = lf.shape[0]
    nblk = T // batch // tb
    const = lambda b, i: (0, 0)
    tok = lambda block: pl.BlockSpec((tb, HG_WIDTH), lambda b, i: (b * nblk + i, block))
    per_head = lambda block: [
        pl.BlockSpec((tb, HG_DK), lambda b, i, c=block * HG_HEADS + h: (b * nblk + i, c))
        for h in range(HG_HEADS)]
    code = jnp.asarray(_score_masks())
    return pl.pallas_call(
        functools.partial(_hgrn2_kernel, tb=tb),
        grid=(batch, nblk),
        in_specs=per_head(PK_Q) + per_head(PK_K) + per_head(0) + [
            tok(PK_V), tok(PK_GS),
            pl.BlockSpec((1, HG_WIDTH), const),
            pl.BlockSpec(code.shape, const)],
        out_specs=tok(0),
        out_shape=jax.ShapeDtypeStruct((T, HG_WIDTH), BF),
        scratch_shapes=[pltpu.VMEM((HG_HEADS, HG_DK, HG_DK), F32),
                        pltpu.VMEM((8, HG_HEADS, tb, HG_DK), BF),
                        pltpu.VMEM((tb // CHUNK, GROUP, HG_WIDTH), F32),
                        pltpu.VMEM((HG_HEADS, CHUNK, CHUNK), BF),
                        pltpu.VMEM((HG_HEADS, HG_DK, HG_DK), BF),
                        pltpu.VMEM((HG_HEADS, CHUNK, HG_DK), F32)],
        compiler_params=pltpu.CompilerParams(
            dimension_semantics=("arbitrary", "arbitrary"), vmem_limit_bytes=VMEM_LIMIT),
        name="hgrn2",
    )(*([pk] * (2 * HG_HEADS)), *([lf] * HG_HEADS), pk, pk, gn, code)


MASK_BIAS = -1e30
PAIR = LANES // ATT_HEAD_DIM


def _swa_bias():
    W = WINDOW
    i = np.arange(W)[:, None]
    j = np.arange(2 * W)[None, :]
    vis = (j > i) & (j <= i + W)
    first = vis & (j >= W)
    return np.where(np.stack([first, vis]), 0.0, MASK_BIAS).astype(np.float32)


def _swa_kernel(sink_ref, bias_ref, q_ref, kp_ref, kc_ref, vp_ref, vc_ref, o_ref,
                s_ref, p_ref, inv_ref):
    W = WINDOW
    lane = lax.broadcasted_iota(jnp.int32, (2 * W, LANES), 1)
    low = lane < ATT_HEAD_DIM
    low_q = lax.broadcasted_iota(jnp.int32, (W, LANES), 1) < ATT_HEAD_DIM

    def placed(t, hk):
        if hk % PAIR == 0:
            lo = jnp.where(low, t, jnp.zeros_like(t))
            return lo, pltpu.roll(lo, ATT_HEAD_DIM, axis=1)
        hi = jnp.where(low, jnp.zeros_like(t), t)
        return pltpu.roll(hi, ATT_HEAD_DIM, axis=1), hi

    def qk_stage(hk):
        tile = slice((hk // PAIR) * LANES, (hk // PAIR + 1) * LANES)
        k_lo, k_hi = placed(jnp.concatenate([kp_ref[:, tile], kc_ref[:, tile]], axis=0), hk)
        for g in range(ATT_GROUP):
            hq = hk * ATT_GROUP + g
            q2 = q_ref[:, (hq // PAIR) * LANES:(hq // PAIR + 1) * LANES]
            s_ref[hq] = _dot_nt(q2, k_lo if hq % PAIR == 0 else k_hi) + bias_ref[0]

    def softmax_stage(hk):
        for g in range(ATT_GROUP):
            hq = hk * ATT_GROUP + g
            s = s_ref[hq]
            sink = jnp.full((W, 1), sink_ref[hq], F32) * LOG2E
            m = jnp.maximum(jnp.max(s, axis=-1, keepdims=True), sink)
            p = jnp.exp2(s - m)
            p_ref[hq] = p.astype(BF)
            inv = 1.0 / (jnp.sum(p, axis=-1, keepdims=True) + jnp.exp2(sink - m))
            inv_ref[hq] = jnp.broadcast_to(inv, (W, LANES))

    def pv_stage(hk):
        tile = slice((hk // PAIR) * LANES, (hk // PAIR + 1) * LANES)
        v_lo, v_hi = placed(jnp.concatenate([vp_ref[:, tile], vc_ref[:, tile]], axis=0), hk)
        for pair in range(ATT_GROUP // PAIR):
            hq = hk * ATT_GROUP + pair * PAIR
            o2 = _dot(p_ref[hq], v_lo) + _dot(p_ref[hq + 1], v_hi)
            o2 = o2 * jnp.where(low_q, inv_ref[hq], inv_ref[hq + 1])
            o_ref[:, (hq // PAIR) * LANES:(hq // PAIR + 1) * LANES] = o2.astype(BF)

    n = ATT_KV_HEADS
    for step in range(n + 2):
        if step < n:
            qk_stage(step)
        if 1 <= step < n + 1:
            softmax_stage(step - 1)
        if step >= 2:
            pv_stage(step - 2)


def _swa(sinks, pk, batch):
    T = pk.shape[0]
    W = WINDOW
    nblk = T // batch // W
    kv_cur = lambda block: pl.BlockSpec((W, ATT_KV_WIDTH), lambda b, n: (b * nblk + n, block))
    kv_prev = lambda block: pl.BlockSpec(
        (W, ATT_KV_WIDTH), lambda b, n: (b * nblk + jnp.maximum(n - 1, 0), block))
    bias = jnp.asarray(_swa_bias())
    return pl.pallas_call(
        _swa_kernel,
        grid=(batch, nblk),
        in_specs=[pl.BlockSpec(memory_space=pltpu.SMEM),
                  pl.BlockSpec((1, W, 2 * W), lambda b, n: (jnp.minimum(n, 1), 0, 0)),
                  pl.BlockSpec((W, ATT_WIDTH), lambda b, n: (b * nblk + n, PK_AQ)),
                  kv_prev(PK_AK), kv_cur(PK_AK), kv_prev(PK_AV), kv_cur(PK_AV)],
        out_specs=pl.BlockSpec((W, ATT_WIDTH), lambda b, n: (b * nblk + n, 0)),
        out_shape=jax.ShapeDtypeStruct((T, ATT_WIDTH), BF),
        scratch_shapes=[pltpu.VMEM((ATT_Q_HEADS, W, 2 * W), F32),
                        pltpu.VMEM((ATT_Q_HEADS, W, 2 * W), BF),
                        pltpu.VMEM((ATT_Q_HEADS, W, LANES), F32)],
        compiler_params=pltpu.CompilerParams(
            dimension_semantics=("arbitrary", "arbitrary"), vmem_limit_bytes=VMEM_LIMIT),
        name="swa",
    )(sinks, bias, pk, pk, pk, pk, pk)


def _merge_kernel(x_ref, oa_ref, ob_ref, ga_ref, gb_ref, wpa_ref, wpb_ref, wo_ref, o_ref):
    ya = _dot(oa_ref[...], wpa_ref[...])
    yb = _dot(ob_ref[...], wpb_ref[...])
    mix = ga_ref[...].astype(F32) * ya + gb_ref[...].astype(F32) * yb
    o_ref[...] = x_ref[...] + _dot(mix.astype(BF), wo_ref[...])


def _merge(x2d, oa, ob, pk, wpa, wpb, wo, tm=1024):
    T = x2d.shape[0]
    const = lambda i: (0, 0)
    tok = lambda block: pl.BlockSpec((tm, D_MODEL), lambda i: (i, block))
    wsp = pl.BlockSpec((D_MODEL, D_MODEL), const)
    return pl.pallas_call(
        _merge_kernel,
        grid=(T // tm,),
        in_specs=[tok(0), tok(0), tok(0), tok(PK_GA), tok(PK_GB), wsp, wsp, wsp],
        out_specs=tok(0),
        out_shape=jax.ShapeDtypeStruct((T, D_MODEL), F32),
        compiler_params=pltpu.CompilerParams(
            dimension_semantics=("arbitrary",), vmem_limit_bytes=VMEM_LIMIT),
        name="merge",
    )(x2d, oa, ob, pk, pk, wpa, wpb, wo)


FFN_COL_CHUNK = FFN_HIDDEN


def _ffn_kernel(x_ref, g_ref, wg_ref, wu_ref, wd_ref, fg_ref, o_ref, act_ref, *, final):
    x = x_ref[...]
    h = x * lax.rsqrt(jnp.mean(x * x, axis=-1, keepdims=True) + EPS) * g_ref[...]
    h = h.astype(BF)
    for c in range(FFN_HIDDEN // FFN_COL_CHUNK):
        cols = slice(c * FFN_COL_CHUNK, (c + 1) * FFN_COL_CHUNK)
        gate = _dot(h, wg_ref[:, cols])
        up = _dot(h, wu_ref[:, cols])
        act_ref[:, cols] = (gate * _sigmoid(gate) * up).astype(BF)
    y = x + _dot(act_ref[...], wd_ref[...])
    if final:
        y = y * lax.rsqrt(jnp.mean(y * y, axis=-1, keepdims=True) + EPS) * fg_ref[...]
    o_ref[...] = y


def _ffn(x2d, g, wg, wu, wd, fg, final, tm=512):
    T = x2d.shape[0]
    row = lambda i: (i, 0)
    const = lambda i: (0, 0)
    tok = pl.BlockSpec((tm, D_MODEL), row)
    vec = pl.BlockSpec((1, D_MODEL), const)
    single = pl.Buffered(1)
    return pl.pallas_call(
        functools.partial(_ffn_kernel, final=final),
        grid=(T // tm,),
        in_specs=[tok, vec,
                  pl.BlockSpec((D_MODEL, FFN_HIDDEN), const, pipeline_mode=single),
                  pl.BlockSpec((D_MODEL, FFN_HIDDEN), const, pipeline_mode=single),
                  pl.BlockSpec((FFN_HIDDEN, D_MODEL), const, pipeline_mode=single),
                  vec],
        out_specs=tok,
        out_shape=jax.ShapeDtypeStruct((T, D_MODEL), F32),
        scratch_shapes=[pltpu.VMEM((tm, FFN_HIDDEN), BF)],
        compiler_params=pltpu.CompilerParams(
            dimension_semantics=("arbitrary",), vmem_limit_bytes=VMEM_LIMIT),
        name="ffn_final" if final else "ffn",
    )(x2d, g, wg, wu, wd, fg)


def _rope_tables(seq):
    inv = ROPE_THETA ** (-jnp.arange(ROPE_HALF, dtype=F32) * 2.0 / ROPE_DIM)
    ang = jnp.arange(seq, dtype=F32)[:, None] * inv[None, :]
    cos, sin = jnp.cos(ang), jnp.sin(ang)
    ones = jnp.ones((seq, ATT_HEAD_DIM - ROPE_DIM), F32)
    zeros = jnp.zeros((seq, ATT_HEAD_DIM - ROPE_DIM), F32)
    zh = jnp.zeros_like(sin)
    cos_h = jnp.concatenate([cos, cos, ones], axis=1)
    s1_h = jnp.concatenate([zh, sin, zeros], axis=1)
    s2_h = jnp.concatenate([-sin, zh, zeros], axis=1)
    rep = LANES // ATT_HEAD_DIM
    return jnp.concatenate([jnp.tile(t, (1, rep)) for t in (cos_h, s1_h, s2_h)], axis=1)


def kernel(x, norm1, w_in, lb_logits, hg_norm, attn_sinks, w_pa, w_pb, w_o,
           norm2, w_gate, w_up, w_down, final_norm):
    B, S, D = x.shape
    T = B * S
    rope_t = _rope_tables(S)
    layer_ids = jnp.arange(DEPTH)[None, :, None]
    sel_all = ((layer_ids >= 1) & (layer_ids <= jnp.arange(DEPTH)[:, None, None])).astype(F32)

    x2d = x.reshape(T, D)
    for l in range(DEPTH):
        pk, lf = _in_proj(x2d, norm1[l][None, :], w_in[l].astype(BF), lb_logits, sel_all[l],
                          rope_t, S)
        o_hg = _hgrn2(pk, lf, hg_norm[l][None, :], B)
        o_at = _swa(attn_sinks[l], pk, B)
        x2d = _merge(x2d, o_hg, o_at, pk,
                     w_pa[l].astype(BF), w_pb[l].astype(BF), w_o[l].astype(BF))
        x2d = _ffn(x2d, norm2[l][None, :], w_gate[l].astype(BF), w_up[l].astype(BF),
                   w_down[l].astype(BF), final_norm[None, :], final=(l == DEPTH - 1))
    return x2d.reshape(B, S, D)
```
